```python
import math
import jax
import jax.numpy as jnp
from jax import lax
import numpy as np

D_MODEL = 1024
BATCH = 4
SEQ = 8192
DEPTH = 1

CTX_LEN = 256
GRID_W = 64

HY_WIDTH = 512
HY_BANDS = 16
HY_EMB = 2 * HY_BANDS + 1
HY_FILTER_ORDER = 64
HY_DECAY_TARGET = 1e-2
HY_DECAY_SHORT = 0.3
HY_DECAY_LONG = 1.5

RW_HEADS = 8
RW_HEAD = 64
RW_WIDTH = RW_HEADS * RW_HEAD
LORA_W = 64
LORA_A = 64
LORA_G = 128
RW_LN_EPS = 64e-5
SCAN_DIRECTIONS = (False, True)

HY_COLS = 3 * HY_WIDTH
RW_COLS = 3 * RW_WIDTH + LORA_W + LORA_A + LORA_G
GATE_COLS = 2 * D_MODEL
IN_COLS = HY_COLS + RW_COLS + GATE_COLS
RW_SPLITS = (RW_WIDTH, 2 * RW_WIDTH, 3 * RW_WIDTH, 3 * RW_WIDTH + LORA_W, 3 * RW_WIDTH + LORA_W + LORA_A)
STATE_LO = RW_WIDTH
STATE_HI = 3 * RW_WIDTH + LORA_W + LORA_A

N_EXPERTS = 16
D_EXPERT = 2048
EC_CAPACITY = 2
NORM_EPS = 1e-6

kernel_name = 'hyena_rwkv7_ec_moe_diffusion_block'


def rmsnorm(x, g):
    x32 = x.astype(jnp.float32)
    y = x32 * lax.rsqrt(jnp.mean(x32 * x32, axis=-1, keepdims=True) + NORM_EPS)
    return y.astype(x.dtype) * g


def modulation(cvec, p):
    return jnp.split(jax.nn.silu(cvec) @ p['mod_w'] + p['mod_b'], 6, axis=-1)


def shift_prev(z):
    return jnp.pad(z, ((0, 0), (1, 0), (0, 0)))[:, :-1]


def shift_next(z):
    return jnp.pad(z, ((0, 0), (0, 1), (0, 0)))[:, 1:]


def centred_mix(z, mu):
    return z + mu[0] * (shift_prev(z) - z) + mu[1] * (shift_next(z) - z)


def heads(z):
    return z.reshape(z.shape[:-1] + (RW_HEADS, RW_HEAD))


def hyena_filter_spectrum(L, p):
    f32 = jnp.float32
    t = jnp.linspace(0.0, 1.0, L, dtype=f32)[:, None]
    ang = (2.0 * math.pi / L) * jnp.arange(L, dtype=f32)[:, None]
    bands = jnp.linspace(1e-4, HY_BANDS - 1, HY_BANDS, dtype=f32)
    feat = jnp.concatenate([t, jnp.cos(bands * ang), -jnp.sin(bands * ang)], axis=-1)
    freq = p['hy_freq'].astype(f32)
    hid = jnp.sin(freq * (feat @ p['hy_ffn_w1'].astype(f32) + p['hy_ffn_b1'].astype(f32)))
    hid = jnp.sin(freq * (hid @ p['hy_ffn_w2'].astype(f32) + p['hy_ffn_b2'].astype(f32)))
    filt = (hid @ p['hy_ffn_w3'].astype(f32)).reshape(L, 2, HY_WIDTH)
    deltas = jnp.abs(jnp.linspace(math.log(HY_DECAY_TARGET) / HY_DECAY_LONG,
                                  math.log(HY_DECAY_TARGET) / HY_DECAY_SHORT, HY_WIDTH, dtype=f32))
    filt = filt * jnp.exp(-t[:, :, None] * deltas)
    circ = jnp.concatenate([filt[:, 0], jnp.zeros((1, HY_WIDTH), f32), filt[:0:-1, 1]], axis=0)
    circ = circ / jnp.sum(jnp.abs(circ), axis=0, keepdims=True)
    return jnp.fft.rfft(circ, axis=0)


def long_conv(z, spec):
    L = z.shape[1]
    zf = jnp.fft.rfft(z.astype(jnp.float32), n=2 * L, axis=1)
    return jnp.fft.irfft(zf * spec, n=2 * L, axis=1)[:, :L].astype(z.dtype)


def hyena_branch(cols, p):
    w = p['hy_conv_w']
    u = w[0] * shift_prev(cols) + w[1] * cols + w[2] * shift_next(cols) + p['hy_conv_b']
    x0, x1, v = jnp.split(u, 3, axis=-1)
    z = v * x1
    z = long_conv(z, hyena_filter_spectrum(cols.shape[1], p)) + p['hy_bias'] * z
    return z * x0


def rwkv_direction(k, lw, la, p, d):
    w = -jax.nn.softplus(-(p['rw_w0'][d] + jnp.tanh(lw) @ p['rw_w_up'][d])) - 0.5
    decay = jnp.exp(-jnp.exp(w.astype(jnp.float32)))
    a = jax.nn.sigmoid(p['rw_a0'][d] + la @ p['rw_a_up'][d])
    kk = heads(k * p['rw_k_k']).astype(jnp.float32)
    kk = kk * lax.rsqrt(jnp.maximum(jnp.sum(kk * kk, axis=-1, keepdims=True), 1e-24))
    k_mod = heads(k * (1.0 + (a - 1.0) * p['rw_k_a']))
    return heads(decay), k_mod, -kk, kk * heads(a).astype(jnp.float32)


def rwkv_scan(decay, k, v, a_vec, b_vec, r, state0, reverse):
    emit = r is not None
    xs = (decay, k, v, a_vec, b_vec) + ((r,) if emit else ())
    xs = tuple(jnp.moveaxis(t.astype(jnp.float32), 1, 0) for t in xs)

    def step(S, inp):
        w_t, k_t, v_t, a_t, b_t = inp[:5]
        sa = jnp.einsum('bhij,bhj->bhi', S, a_t)
        S = S * w_t[:, :, None, :] + sa[..., None] * b_t[:, :, None, :] + v_t[..., None] * k_t[:, :, None, :]
        y = jnp.einsum('bhij,bhj->bhi', S, inp[5]) if emit else None
        return S, y

    S, ys = lax.scan(step, state0, xs, reverse=reverse)
    return S, (jnp.moveaxis(ys, 0, 1) if emit else None)


def head_groupnorm(y, g, b):
    y32 = y.astype(jnp.float32)
    mu = jnp.mean(y32, axis=-1, keepdims=True)
    var = jnp.mean(jnp.square(y32 - mu), axis=-1, keepdims=True)
    yn = (y32 - mu) * lax.rsqrt(var + RW_LN_EPS)
    return yn.reshape(y.shape[:-2] + (RW_WIDTH,)).astype(g.dtype) * g + b


def rwkv_branch(cols, p, states0):
    z = centred_mix(cols, p['rw_mu'])
    r, k, v, lw, la, lg = jnp.split(z, RW_SPLITS, axis=-1)
    r_h, v_h = heads(r), heads(v)
    outs, bonuses, finals = [], [], []
    for d, reverse in enumerate(SCAN_DIRECTIONS):
        decay, k_mod, a_vec, b_vec = rwkv_direction(k, lw, la, p, d)
        S, y_d = rwkv_scan(decay, k_mod, v_h, a_vec, b_vec, r_h, states0[d], reverse)
        outs.append(y_d)
        finals.append(S)
        bonuses.append(jnp.sum(r_h * k_mod * p['rw_r_k'], axis=-1, keepdims=True) * v_h)
    y = head_groupnorm(outs[0] + outs[1], p['rw_ln_g'], p['rw_ln_b'])
    y = y + (bonuses[0] + bonuses[1]).reshape(cols.shape[:-1] + (RW_WIDTH,))
    g = jax.nn.sigmoid(lg) @ p['rw_g_up']
    return y * g, finals


def rwkv_context_states(cols, p, state0):
    z = centred_mix(cols, p['rw_mu'][:, STATE_LO:STATE_HI])
    k, v, lw, la = jnp.split(z, (RW_WIDTH, 2 * RW_WIDTH, 2 * RW_WIDTH + LORA_W), axis=-1)
    v_h = heads(v)
    finals = []
    for d, reverse in enumerate(SCAN_DIRECTIONS):
        decay, k_mod, a_vec, b_vec = rwkv_direction(k, lw, la, p, d)
        S, _ = rwkv_scan(decay, k_mod, v_h, a_vec, b_vec, None, state0, reverse)
        finals.append(S)
    return finals


def token_mixer(h, p, states0):
    cols = h @ p['w_in']
    hy_cols, rw_cols, gate_cols = jnp.split(cols, (HY_COLS, HY_COLS + RW_COLS), axis=-1)
    y_hy = hyena_branch(hy_cols, p) @ p['hy_proj']
    y_rw, finals = rwkv_branch(rw_cols, p, states0)
    g_hy, g_rw = jnp.split(jax.nn.sigmoid(gate_cols), 2, axis=-1)
    mixed = (g_hy * y_hy + g_rw * (y_rw @ p['rw_proj'])) @ p['w_out']
    return mixed, finals


def expert_choice_moe(h, p):
    B, T, D = h.shape
    cap = EC_CAPACITY * T // N_EXPERTS
    affinity = jax.nn.softmax((h @ p['router_w']).astype(jnp.float32), axis=-1)
    gates, idx = lax.top_k(jnp.swapaxes(affinity, 1, 2), cap)
    xe = jax.vmap(lambda hb, ib: hb[ib])(h, idx)
    hid = jax.nn.silu(jnp.einsum('becd,edf->becf', xe, p['exp_w1'])) * jnp.einsum('becd,edf->becf', xe, p['exp_w3'])
    ye = jnp.einsum('becf,efd->becd', hid, p['exp_w2']) * gates[..., None].astype(h.dtype)
    return jax.vmap(lambda ib, yb: jnp.zeros((T, D), yb.dtype).at[ib.reshape(-1)].add(yb.reshape(-1, D)))(idx, ye)


def layer(x, ctx, c, c_ctx, p, ctx_out):
    m_lat = modulation(c[:, None, :], p)
    m_ctx = modulation(c_ctx, p)
    zero_state = jnp.zeros((ctx.shape[0], RW_HEADS, RW_HEAD, RW_HEAD), jnp.float32)
    h_ctx = rmsnorm(ctx, p['norm1_pre']) * (1.0 + m_ctx[1]) + m_ctx[0]
    if ctx_out:
        mixed_ctx, ctx_states = token_mixer(h_ctx, p, (zero_state, zero_state))
        ctx = ctx + m_ctx[2] * rmsnorm(mixed_ctx, p['norm1_post'])
        h2_ctx = rmsnorm(ctx, p['norm2_pre']) * (1.0 + m_ctx[4]) + m_ctx[3]
        ctx = ctx + m_ctx[5] * rmsnorm(expert_choice_moe(h2_ctx, p), p['norm2_post'])
    else:
        cols = h_ctx @ p['w_in'][:, HY_COLS + STATE_LO:HY_COLS + STATE_HI]
        ctx_states = rwkv_context_states(cols, p, zero_state)
    h = rmsnorm(x, p['norm1_pre']) * (1.0 + m_lat[1]) + m_lat[0]
    mixed, _ = token_mixer(h, p, ctx_states)
    x = x + m_lat[2] * rmsnorm(mixed, p['norm1_post'])
    h2 = rmsnorm(x, p['norm2_pre']) * (1.0 + m_lat[4]) + m_lat[3]
    x = x + m_lat[5] * rmsnorm(expert_choice_moe(h2, p), p['norm2_post'])
    return x, ctx


def setup_inputs(seed: int = 0) -> dict:
    key = jax.random.key(seed)
    ks = iter(jax.random.split(key, 64))
    f32 = jnp.float32

    def nrm(shape, scale=1.0):
        return scale * jax.random.normal(next(ks), shape, f32)

    def unif(shape, lo, hi):
        return jax.random.uniform(next(ks), shape, f32, lo, hi)

    L, D, E = DEPTH, D_MODEL, N_EXPERTS
    return {
        'x': nrm((BATCH, SEQ, D)),
        'c': nrm((BATCH, D)),
        'ctx': nrm((BATCH, CTX_LEN, D)),
        'c_ctx': nrm((D,)),
        'mod_w': nrm((L, D, 6 * D), 0.5 * D ** -0.5),
        'mod_b': nrm((L, 6 * D), 0.02),
        'norm1_pre': 1.0 + nrm((L, D), 0.1),
        'norm1_post': 1.0 + nrm((L, D), 0.1),
        'norm2_pre': 1.0 + nrm((L, D), 0.1),
        'norm2_post': 1.0 + nrm((L, D), 0.1),
        'w_in': nrm((L, D, IN_COLS), D ** -0.5),
        'hy_conv_w': nrm((L, 3, HY_COLS), 0.5),
        'hy_conv_b': nrm((L, HY_COLS), 0.02),
        'hy_ffn_w1': nrm((L, HY_EMB, HY_FILTER_ORDER), HY_EMB ** -0.5),
        'hy_ffn_b1': nrm((L, HY_FILTER_ORDER), 0.1),
        'hy_ffn_w2': nrm((L, HY_FILTER_ORDER, HY_FILTER_ORDER), HY_FILTER_ORDER ** -0.5),
        'hy_ffn_b2': nrm((L, HY_FILTER_ORDER), 0.1),
        'hy_ffn_w3': nrm((L, HY_FILTER_ORDER, 2 * HY_WIDTH), HY_FILTER_ORDER ** -0.5),
        'hy_freq': 1.0 + nrm((L, HY_FILTER_ORDER), 0.1),
        'hy_bias': nrm((L, HY_WIDTH), 0.5),
        'hy_proj': nrm((L, HY_WIDTH, D), HY_WIDTH ** -0.5),
        'rw_mu': unif((L, 2, RW_COLS), 0.0, 0.5),
        'rw_w0': unif((L, 2, RW_WIDTH), -6.0, 0.0),
        'rw_w_up': nrm((L, 2, LORA_W, RW_WIDTH), 0.1),
        'rw_a0': nrm((L, 2, RW_WIDTH), 0.3),
        'rw_a_up': nrm((L, 2, LORA_A, RW_WIDTH), 0.1),
        'rw_g_up': nrm((L, LORA_G, RW_WIDTH), LORA_G ** -0.5),
        'rw_k_k': 0.85 + nrm((L, RW_WIDTH), 0.1),
        'rw_k_a': 1.0 + nrm((L, RW_WIDTH), 0.1),
        'rw_r_k': nrm((L, RW_HEADS, RW_HEAD), 0.1),
        'rw_ln_g': 1.0 + nrm((L, RW_WIDTH), 0.1),
        'rw_ln_b': nrm((L, RW_WIDTH), 0.02),
        'rw_proj': nrm((L, RW_WIDTH, D), RW_WIDTH ** -0.5),
        'w_out': nrm((L, D, D), D ** -0.5),
        'router_w': nrm((L, D, E), D ** -0.5),
        'exp_w1': nrm((L, E, D, D_EXPERT), D ** -0.5),
        'exp_w3': nrm((L, E, D, D_EXPERT), D ** -0.5),
        'exp_w2': nrm((L, E, D_EXPERT, D), D_EXPERT ** -0.5),
    }


def reference(x, c, ctx, c_ctx, mod_w, mod_b, norm1_pre, norm1_post, norm2_pre, norm2_post, w_in,
              hy_conv_w, hy_conv_b, hy_ffn_w1, hy_ffn_b1, hy_ffn_w2, hy_ffn_b2, hy_ffn_w3, hy_freq,
              hy_bias, hy_proj, rw_mu, rw_w0, rw_w_up, rw_a0, rw_a_up, rw_g_up, rw_k_k, rw_k_a,
              rw_r_k, rw_ln_g, rw_ln_b, rw_proj, w_out, router_w, exp_w1, exp_w3, exp_w2):
    for l in range(DEPTH):
        p = {
            'mod_w': mod_w[l], 'mod_b': mod_b[l],
            'norm1_pre': norm1_pre[l], 'norm1_post': norm1_post[l],
            'norm2_pre': norm2_pre[l], 'norm2_post': norm2_post[l],
            'w_in': w_in[l],
            'hy_conv_w': hy_conv_w[l], 'hy_conv_b': hy_conv_b[l],
            'hy_ffn_w1': hy_ffn_w1[l], 'hy_ffn_b1': hy_ffn_b1[l],
            'hy_ffn_w2': hy_ffn_w2[l], 'hy_ffn_b2': hy_ffn_b2[l],
            'hy_ffn_w3': hy_ffn_w3[l], 'hy_freq': hy_freq[l],
            'hy_bias': hy_bias[l], 'hy_proj': hy_proj[l],
            'rw_mu': rw_mu[l], 'rw_w0': rw_w0[l], 'rw_w_up': rw_w_up[l],
            'rw_a0': rw_a0[l], 'rw_a_up': rw_a_up[l], 'rw_g_up': rw_g_up[l],
            'rw_k_k': rw_k_k[l], 'rw_k_a': rw_k_a[l], 'rw_r_k': rw_r_k[l],
            'rw_ln_g': rw_ln_g[l], 'rw_ln_b': rw_ln_b[l], 'rw_proj': rw_proj[l],
            'w_out': w_out[l], 'router_w': router_w[l],
            'exp_w1': exp_w1[l], 'exp_w3': exp_w3[l], 'exp_w2': exp_w2[l],
        }
        x, ctx = layer(x, ctx, c, c_ctx, p, l < DEPTH - 1)
    return x
```

```python
import functools
import math

import numpy as np
import jax
import jax.numpy as jnp
from jax import lax
from jax.experimental import pallas as pl
from jax.experimental.pallas import tpu as pltpu

F32 = jnp.float32
BF16 = jnp.bfloat16
I32 = jnp.int32
HI = lax.Precision.HIGHEST

D_MODEL = 1024
BATCH = 4
SEQ = 8192
CTX_LEN = 256

HY_WIDTH = 512
HY_BANDS = 16
HY_FILTER_ORDER = 64
HY_DECAY_TARGET = 1e-2
HY_DECAY_SHORT = 0.3
HY_DECAY_LONG = 1.5

RW_HEADS = 8
RW_HEAD = 64
RW_WIDTH = RW_HEADS * RW_HEAD
LORA_W = 64
LORA_A = 64
LORA_G = 128
RW_LN_EPS = 64e-5

HY_COLS = 3 * HY_WIDTH
RW_COLS = 3 * RW_WIDTH + LORA_W + LORA_A + LORA_G
GATE_COLS = 2 * D_MODEL

N_EXPERTS = 16
D_EXPERT = 2048
EC_CAPACITY = 2
NORM_EPS = 1e-6
CAP = EC_CAPACITY * SEQ // N_EXPERTS

NFFT = 2 * SEQ
FFT_R = 128
FFT_H = FFT_R // 2
FFT_J = 8
CHUNK = 64

VMEM_LIMIT = 56 * 1024 * 1024


def _cparams(sem, vmem=None):
    return pltpu.CompilerParams(dimension_semantics=sem, vmem_limit_bytes=vmem)


def _dot_hi(a, b):
    return jnp.dot(a, b, precision=HI, preferred_element_type=F32)


def _dot_nt_hi(a, b):
    return lax.dot_general(a, b, (((1,), (1,)), ((), ())), precision=HI, preferred_element_type=F32)


def _dot_tn_hi(a, b):
    return lax.dot_general(a, b, (((0,), (0,)), ((), ())), precision=HI, preferred_element_type=F32)


def _dot_bf(a, b):
    return jnp.dot(a.astype(BF16), b.astype(BF16), preferred_element_type=F32)


def _rms(x):
    return x * lax.rsqrt(jnp.mean(x * x, axis=-1, keepdims=True) + NORM_EPS)


def _sigmoid(x):
    return 1.0 / (1.0 + jnp.exp(-x))


def _mod_body(c_ref, w_ref, b_ref, o_ref):
    c = c_ref[...]
    o_ref[...] = _dot_hi(c * _sigmoid(c), w_ref[...]) + b_ref[...]


def _modulation(c8, mod_w, mod_b):
    n = mod_w.shape[1]
    tn = 1536
    return pl.pallas_call(
        _mod_body,
        grid=(n // tn,),
        in_specs=[pl.BlockSpec((8, D_MODEL), lambda j: (0, 0)),
                  pl.BlockSpec((D_MODEL, tn), lambda j: (0, j)),
                  pl.BlockSpec((1, tn), lambda j: (0, j))],
        out_specs=pl.BlockSpec((8, tn), lambda j: (0, j)),
        out_shape=jax.ShapeDtypeStruct((8, n), F32),
        compiler_params=_cparams(("arbitrary",)),
        name="modulation",
    )(c8, mod_w, mod_b.reshape(1, n))


def _inproj_body(nw, x_ref, g_ref, sc_ref, sh_ref, *refs):
    w_refs, o_refs, h_ref = refs[:nw], refs[nw:2 * nw], refs[2 * nw]

    @pl.when(pl.program_id(2) == 0)
    def _():
        h = _rms(x_ref[...]) * g_ref[...] * sc_ref[...] + sh_ref[...]
        h_ref[...] = h.astype(BF16)

    h = h_ref[...]
    for w_ref, o_ref in zip(w_refs, o_refs):
        o_ref[...] = jnp.dot(h, w_ref[...], preferred_element_type=F32)


def _in_projection(x, g, scale1p, shift, weights, tm, nsplit):
    bx, tx, _ = x.shape
    nw = len(weights)
    tns = [w.shape[1] // nsplit for w in weights]
    in_specs = [pl.BlockSpec((None, tm, D_MODEL), lambda b, m, n: (b, m, 0)),
                pl.BlockSpec((1, D_MODEL), lambda b, m, n: (0, 0)),
                pl.BlockSpec((None, 1, D_MODEL), lambda b, m, n: (b, 0, 0)),
                pl.BlockSpec((None, 1, D_MODEL), lambda b, m, n: (b, 0, 0))]
    in_specs += [pl.BlockSpec((D_MODEL, tn), lambda b, m, n: (0, n)) for tn in tns]
    out_specs = [pl.BlockSpec((None, tm, tn), lambda b, m, n: (b, m, n)) for tn in tns]
    out_shape = [jax.ShapeDtypeStruct((bx, tx, w.shape[1]), F32) for w in weights]
    return pl.pallas_call(
        functools.partial(_inproj_body, nw),
        grid=(bx, tx // tm, nsplit),
        in_specs=in_specs,
        out_specs=out_specs,
        out_shape=out_shape,
        scratch_shapes=[pltpu.VMEM((tm, D_MODEL), BF16)],
        compiler_params=_cparams(("arbitrary", "arbitrary", "arbitrary"), VMEM_LIMIT),
        name="in_projection",
    )(x, g.reshape(1, D_MODEL), scale1p, shift, *weights)


def _neighbours(cur, pv_ref, nx_ref):
    tt = cur.shape[0]
    t = pl.program_id(1)
    first = (t > 0).astype(F32)
    last = (t < pl.num_programs(1) - 1).astype(F32)
    prev_row = pv_ref[7:8, :] * first
    next_row = nx_ref[0:1, :] * last
    rows = lax.broadcasted_iota(I32, cur.shape, 0)
    prev = jnp.where(rows == 0, prev_row, pltpu.roll(cur, 1, 0))
    nxt = jnp.where(rows == tt - 1, next_row, pltpu.roll(cur, tt - 1, 0))
    return prev, nxt


def _halo_specs(tt, t_total, width):
    r8 = tt // 8
    last8 = t_total // 8 - 1
    return [pl.BlockSpec((None, tt, width), lambda b, t: (b, t, 0)),
            pl.BlockSpec((None, 8, width), lambda b, t: (b, jnp.maximum(t * r8 - 1, 0), 0)),
            pl.BlockSpec((None, 8, width), lambda b, t: (b, jnp.minimum((t + 1) * r8, last8), 0))]


def _hy_pre_body(cur_ref, pv_ref, nx_ref, w_ref, b_ref, z_ref, x0_ref):
    cur = cur_ref[...]
    prev, nxt = _neighbours(cur, pv_ref, nx_ref)
    u = w_ref[0:1, :] * prev + w_ref[1:2, :] * cur + w_ref[2:3, :] * nxt + b_ref[...]
    z_ref[...] = u[:, 2 * HY_WIDTH:] * u[:, HY_WIDTH:2 * HY_WIDTH]
    x0_ref[...] = u[:, :HY_WIDTH]


def _hyena_pre(hy_cols, conv_w, conv_b):
    tt = 512
    spec_o = pl.BlockSpec((None, tt, HY_WIDTH), lambda b, t: (b, t, 0))
    return pl.pallas_call(
        _hy_pre_body,
        grid=(BATCH, SEQ // tt),
        in_specs=_halo_specs(tt, SEQ, HY_COLS) + [
            pl.BlockSpec((3, HY_COLS), lambda b, t: (0, 0)),
            pl.BlockSpec((1, HY_COLS), lambda b, t: (0, 0))],
        out_specs=[spec_o, spec_o],
        out_shape=[jax.ShapeDtypeStruct((BATCH, SEQ, HY_WIDTH), F32)] * 2,
        compiler_params=_cparams(("arbitrary", "arbitrary"), VMEM_LIMIT),
        name="hyena_pre",
    )(hy_cols, hy_cols, hy_cols, conv_w, conv_b.reshape(1, HY_COLS))


def _filt_body(bands_ref, w1t_ref, w1c_ref, w1s_ref, b1_ref, w2_ref, b2_ref, w3_ref, fr_ref, dl_ref,
               f_ref, s_ref):
    i = pl.program_id(0)
    tt = f_ref.shape[0]
    rowi = lax.broadcasted_iota(I32, (tt, 1), 0) + i * tt
    rowf = rowi.astype(F32)
    tpos = rowf * (1.0 / (SEQ - 1))
    arg = (rowf * (2.0 * math.pi / SEQ)) * bands_ref[...]
    pre = tpos * w1t_ref[...] + _dot_hi(jnp.cos(arg), w1c_ref[...]) - _dot_hi(jnp.sin(arg), w1s_ref[...]) + b1_ref[...]
    freq = fr_ref[...]
    hid = jnp.sin(freq * pre)
    hid = jnp.sin(freq * (_dot_hi(hid, w2_ref[...]) + b2_ref[...]))
    filt = _dot_hi(hid, w3_ref[...]) * jnp.exp(-tpos * dl_ref[...])
    lane = lax.broadcasted_iota(I32, filt.shape, 1)
    rows = lax.broadcasted_iota(I32, filt.shape, 0) + i * tt
    filt = jnp.where((rows == 0) & (lane >= HY_WIDTH), 0.0, filt)
    f_ref[...] = filt

    @pl.when(i == 0)
    def _():
        s_ref[...] = jnp.zeros_like(s_ref)

    s_ref[...] += jnp.sum(jnp.abs(filt), axis=0, keepdims=True)


def _pad2(a, rows, cols):
    return jnp.zeros((rows, cols), F32).at[:a.shape[0], :a.shape[1]].set(a)


def _hyena_filter(w1, b1, w2, b2, w3, freq):
    tt = 512
    fo = HY_FILTER_ORDER
    bands = np.zeros((1, 128), np.float32)
    bands[0, :HY_BANDS] = np.linspace(1e-4, HY_BANDS - 1, HY_BANDS, dtype=np.float32)
    deltas = np.abs(np.linspace(math.log(HY_DECAY_TARGET) / HY_DECAY_LONG,
                                math.log(HY_DECAY_TARGET) / HY_DECAY_SHORT, HY_WIDTH, dtype=np.float32))
    dl = np.concatenate([deltas, deltas])[None, :].astype(np.float32)
    args = (jnp.asarray(bands),
            _pad2(w1[0:1], 1, 128),
            _pad2(w1[1:1 + HY_BANDS], 128, 128),
            _pad2(w1[1 + HY_BANDS:], 128, 128),
            _pad2(b1[None], 1, 128),
            _pad2(w2, 128, 128),
            _pad2(b2[None], 1, 128),
            _pad2(w3, 128, 2 * HY_WIDTH),
            _pad2(freq[None], 1, 128),
            jnp.asarray(dl))
    del fo
    full = lambda a: pl.BlockSpec(a.shape, lambda i: (0, 0))
    return pl.pallas_call(
        _filt_body,
        grid=(SEQ // tt,),
        in_specs=[full(a) for a in args],
        out_specs=[pl.BlockSpec((tt, 2 * HY_WIDTH), lambda i: (i, 0)),
                   pl.BlockSpec((1, 2 * HY_WIDTH), lambda i: (0, 0))],
        out_shape=[jax.ShapeDtypeStruct((SEQ, 2 * HY_WIDTH), F32),
                   jax.ShapeDtypeStruct((1, 2 * HY_WIDTH), F32)],
        compiler_params=_cparams(("arbitrary",)),
        name="hyena_filter",
    )(*args)


def _fft_tables():
    k = np.arange(FFT_R)
    n = np.arange(FFT_H)
    th = 2.0 * np.pi * np.outer(k, n) / FFT_R
    f1 = np.stack([np.cos(th), -np.sin(th)])
    eye = np.eye(FFT_J)
    g1 = np.einsum('pkn,jm->pkjnm', f1, eye).reshape(2 * FFT_R * FFT_J, FFT_H * FFT_J)
    f3 = np.stack([np.cos(th.T), -np.sin(th.T)]) / NFFT
    g3 = np.einsum('pnk,jm->njpkm', f3, eye).reshape(FFT_H * FFT_J, 2 * FFT_R * FFT_J)
    th2 = 2.0 * np.pi * np.outer(k, k) / FFT_R
    c, s = np.cos(th2), np.sin(th2)
    fc = np.block([[c, s], [-s, c]])
    fi = np.block([[c, -s], [s, c]])
    return tuple(jnp.asarray(a, F32) for a in (g1, g3, fc, fi))


def _twiddle(rows_k, n2):
    ph = ((rows_k * n2) & (NFFT - 1)).astype(F32) * (2.0 * math.pi / NFFT)
    return jnp.cos(ph), jnp.sin(ph)


def _fft1_body(z_ref, g1_ref, o_ref):
    nb = pl.program_id(1)
    c = z_ref.shape[-1]
    z = z_ref[...].reshape(FFT_H * FFT_J, c)
    a = _dot_hi(g1_ref[...], z)
    half = FFT_R * FFT_J
    ar, ai = a[:half], a[half:]
    r = lax.broadcasted_iota(I32, (half, 1), 0)
    cs, sn = _twiddle(r >> 3, nb * FFT_J + (r & (FFT_J - 1)))
    o_ref[0] = (ar * cs + ai * sn).reshape(FFT_R, FFT_J, c)
    o_ref[1] = (ai * cs - ar * sn).reshape(FFT_R, FFT_J, c)


def _fft_stage1(z4, g1):
    bz, _, _, c = z4.shape
    return pl.pallas_call(
        _fft1_body,
        grid=(bz, FFT_R // FFT_J),
        in_specs=[pl.BlockSpec((None, FFT_H, FFT_J, c), lambda b, n: (b, 0, n, 0)),
                  pl.BlockSpec(g1.shape, lambda b, n: (0, 0))],
        out_specs=pl.BlockSpec((None, 2, FFT_R, FFT_J, c), lambda b, n: (b, 0, 0, n, 0)),
        out_shape=jax.ShapeDtypeStruct((bz, 2, FFT_R, FFT_R, c), F32),
        compiler_params=_cparams(("arbitrary", "arbitrary"), VMEM_LIMIT),
        name="fft_stage1",
    )(z4, g1)


def _filt_spec_body(a_ref, fc_ref, h_ref):
    x = _dot_hi(fc_ref[...], jnp.concatenate([a_ref[0], a_ref[1]], axis=0))
    xr, xi = x[:FFT_R], x[FFT_R:]
    h_ref[0] = xr[:, :HY_WIDTH] + xr[:, HY_WIDTH:]
    h_ref[1] = xi[:, :HY_WIDTH] - xi[:, HY_WIDTH:]


def _filter_spectrum(a, fc):
    return pl.pallas_call(
        _filt_spec_body,
        grid=(FFT_R,),
        in_specs=[pl.BlockSpec((None, 2, None, FFT_R, 2 * HY_WIDTH), lambda k: (0, 0, k, 0, 0)),
                  pl.BlockSpec(fc.shape, lambda k: (0, 0))],
        out_specs=pl.BlockSpec((None, 2, FFT_R, HY_WIDTH), lambda k: (k, 0, 0, 0)),
        out_shape=jax.ShapeDtypeStruct((FFT_R, 2, FFT_R, HY_WIDTH), F32),
        compiler_params=_cparams(("arbitrary",)),
        name="filter_spectrum",
    )(a, fc)


def _fft_mid_body(a_ref, h_ref, fc_ref, fi_ref, o_ref):
    k1 = pl.program_id(0)
    x = _dot_hi(fc_ref[...], jnp.concatenate([a_ref[0], a_ref[1]], axis=0))
    xr, xi = x[:FFT_R], x[FFT_R:]
    hr, hi = h_ref[0], h_ref[1]
    yr = xr * hr - xi * hi
    yi = xr * hi + xi * hr
    p = _dot_hi(fi_ref[...], jnp.concatenate([yr, yi], axis=0))
    pr, pim = p[:FFT_R], p[FFT_R:]
    cs, sn = _twiddle(lax.broadcasted_iota(I32, (FFT_R, 1), 0), k1)
    o_ref[0] = pr * cs - pim * sn
    o_ref[1] = pr * sn + pim * cs


def _fft_mid(a, h, fc, fi):
    bz = a.shape[0]
    blk = pl.BlockSpec((None, 2, None, FFT_R, HY_WIDTH), lambda k, b: (b, 0, k, 0, 0))
    return pl.pallas_call(
        _fft_mid_body,
        grid=(FFT_R, bz),
        in_specs=[blk,
                  pl.BlockSpec((None, 2, FFT_R, HY_WIDTH), lambda k, b: (k, 0, 0, 0)),
                  pl.BlockSpec(fc.shape, lambda k, b: (0, 0)),
                  pl.BlockSpec(fi.shape, lambda k, b: (0, 0))],
        out_specs=blk,
        out_shape=jax.ShapeDtypeStruct(a.shape, F32),
        compiler_params=_cparams(("arbitrary", "arbitrary")),
        name="fft_mid",
    )(a, h, fc, fi)


def _ifft_body(q_ref, g3_ref, z_ref, x0_ref, inv_ref, hb_ref, o_ref):
    c = HY_WIDTH
    q = q_ref[...].reshape(2 * FFT_R * FFT_J, c)
    conv = _dot_hi(g3_ref[...], q) * inv_ref[...]
    z = z_ref[...].reshape(FFT_H * FFT_J, c)
    x0 = x0_ref[...].reshape(FFT_H * FFT_J, c)
    o_ref[...] = ((conv + hb_ref[...] * z) * x0).reshape(FFT_H, FFT_J, c)


def _ifft_gate(q, g3, z4, x04, inv_norm, hy_bias):
    row = pl.BlockSpec((None, FFT_H, FFT_J, HY_WIDTH), lambda b, n: (b, 0, n, 0))
    vec = pl.BlockSpec((1, HY_WIDTH), lambda b, n: (0, 0))
    return pl.pallas_call(
        _ifft_body,
        grid=(BATCH, FFT_R // FFT_J),
        in_specs=[pl.BlockSpec((None, 2, FFT_R, FFT_J, HY_WIDTH), lambda b, n: (b, 0, 0, n, 0)),
                  pl.BlockSpec(g3.shape, lambda b, n: (0, 0)), row, row, vec, vec],
        out_specs=row,
        out_shape=jax.ShapeDtypeStruct(z4.shape, F32),
        compiler_params=_cparams(("arbitrary", "arbitrary"), VMEM_LIMIT),
        name="ifft_gate",
    )(q, g3, z4, x04, inv_norm, hy_bias)


def _hyena_branch(hy_cols, conv_w, conv_b, w1, b1, w2, b2, w3, freq, hy_bias):
    g1, g3, fc, fi = _fft_tables()
    z, x0 = _hyena_pre(hy_cols, conv_w, conv_b)
    filt, l1 = _hyena_filter(w1, b1, w2, b2, w3, freq)
    inv_norm = 1.0 / (l1[:, :HY_WIDTH] + l1[:, HY_WIDTH:])
    fa = _fft_stage1(filt.reshape(1, FFT_H, FFT_R, 2 * HY_WIDTH), g1)
    h = _filter_spectrum(fa, fc)
    z4 = z.reshape(BATCH, FFT_H, FFT_R, HY_WIDTH)
    za = _fft_stage1(z4, g1)
    q = _fft_mid(za, h, fc, fi)
    out = _ifft_gate(q, g3, z4, x0.reshape(z4.shape), inv_norm, hy_bias.reshape(1, HY_WIDTH))
    return out.reshape(BATCH, SEQ, HY_WIDTH)


def _rw_prep_body(cur_ref, pv_ref, nx_ref, mu_ref, w2_ref, w0_ref, a0_ref, gup_ref, kkw_ref, ka_ref, rk_ref,
                  ones_ref, r_o, v_o, kk_o, e_o, km_o, b_o, bonus_o, g_o):
    cur = cur_ref[...]
    prev, nxt = _neighbours(cur, pv_ref, nx_ref)
    z = cur + mu_ref[0:1, :] * (prev - cur) + mu_ref[1:2, :] * (nxt - cur)
    w_ = RW_WIDTH
    r, k, v = z[:, :w_], z[:, w_:2 * w_], z[:, 2 * w_:3 * w_]
    ll = z[:, 3 * w_:3 * w_ + 128]
    lg = z[:, 3 * w_ + 128:]
    lane = lax.broadcasted_iota(I32, ll.shape, 1)
    tl = jnp.where(lane < LORA_W, jnp.tanh(ll), ll)
    ones = ones_ref[...]
    kkr = k * kkw_ref[...]
    kk = kkr * lax.rsqrt(jnp.maximum(_dot_hi(kkr * kkr, ones), 1e-24))
    bonus = jnp.zeros_like(r)
    for d in range(2):
        proj = _dot_hi(tl, w2_ref[d])
        wl = proj[:, :w_] + w0_ref[d:d + 1, :]
        al = proj[:, w_:] + a0_ref[d:d + 1, :]
        nwl = -wl
        w = -(jnp.maximum(nwl, 0.0) + jnp.log(1.0 + jnp.exp(-jnp.abs(nwl)))) - 0.5
        a = _sigmoid(al)
        km = k * (1.0 + (a - 1.0) * ka_ref[...])
        e_o[d] = jnp.exp(w)
        km_o[d] = km
        b_o[d] = kk * a
        bonus = bonus + _dot_hi(r * km * rk_ref[...], ones)
    r_o[...] = r
    v_o[...] = v
    kk_o[...] = kk
    bonus_o[...] = bonus * v
    g_o[...] = _dot_hi(_sigmoid(lg), gup_ref[...])


def _head_ones():
    h = np.arange(RW_WIDTH) // RW_HEAD
    return jnp.asarray((h[:, None] == h[None, :]).astype(np.float32))


def _rwkv_prep(rw_cols, mu, w_up, a_up, w0, a0, g_up, k_k, k_a, r_k):
    bx, tx, _ = rw_cols.shape
    tt = 256
    w2 = jnp.zeros((2, 128, 2 * RW_WIDTH), F32)
    w2 = w2.at[:, :LORA_W, :RW_WIDTH].set(w_up).at[:, LORA_W:, RW_WIDTH:].set(a_up)
    vec = lambda a: a.reshape(1, RW_WIDTH)
    full2 = lambda a: pl.BlockSpec(a.shape, lambda b, t: (0,) * a.ndim)
    args = (mu, w2, w0, a0, g_up, vec(k_k), vec(k_a), vec(r_k), _head_ones())
    one = pl.BlockSpec((None, tt, RW_WIDTH), lambda b, t: (b, t, 0))
    two = pl.BlockSpec((2, None, tt, RW_WIDTH), lambda b, t: (0, b, t, 0))
    s1 = jax.ShapeDtypeStruct((bx, tx, RW_WIDTH), F32)
    s2 = jax.ShapeDtypeStruct((2, bx, tx, RW_WIDTH), F32)
    return pl.pallas_call(
        _rw_prep_body,
        grid=(bx, tx // tt),
        in_specs=_halo_specs(tt, tx, RW_COLS) + [full2(a) for a in args],
        out_specs=[one, one, one, two, two, two, one, one],
        out_shape=[s1, s1, s1, s2, s2, s2, s1, s1],
        compiler_params=_cparams(("arbitrary", "arbitrary"), VMEM_LIMIT),
        name="rwkv_prep",
    )(rw_cols, rw_cols, rw_cols, *args)


def _rw_scan_body(r_ref, v_ref, kk_ref, e_ref, km_ref, b_ref, s0_ref, y_ref, sf_ref, s_scr):
    d = pl.program_id(0)
    c = pl.program_id(2)

    @pl.when(c == 0)
    def _():
        s_scr[...] = s0_ref[...]

    n = CHUNK
    row = lax.broadcasted_iota(I32, (n, n), 0)
    col = lax.broadcasted_iota(I32, (n, n), 1)
    fwd = d == 0
    later = jnp.where(fwd, row, col)
    earlier = jnp.where(fwd, col, row)
    incl = later >= earlier
    strict = later > earlier
    e = e_ref[...]
    cum = _dot_hi(incl.astype(F32), e)
    tot = jnp.sum(e, axis=0, keepdims=True)
    r, v, kk, km, bv = r_ref[...], v_ref[...], kk_ref[...], km_ref[...], b_ref[...]
    grow = jnp.exp(cum)
    at = -kk * jnp.exp(e - cum)
    bt = bv * grow
    kt = km * grow
    rt = r * jnp.exp(-cum)
    tail = jnp.exp(cum - tot)
    bb = bv * tail
    kb = km * tail
    p_all = jnp.exp(-tot)
    for h in range(RW_HEADS):
        sl = slice(RW_HEAD * h, RW_HEAD * (h + 1))
        a_h, b_h, k_h, r_h, v_h = at[:, sl], bt[:, sl], kt[:, sl], rt[:, sl], v[:, sl]
        ar = jnp.concatenate([a_h, r_h], axis=0)
        gb = _dot_nt_hi(ar, b_h)
        gk = _dot_nt_hi(ar, k_h)
        m_ab = jnp.where(strict, gb[:n], 0.0)
        m_ak = jnp.where(strict, gk[:n], 0.0)
        g_rb = jnp.where(incl, gb[n:], 0.0)
        g_rk = jnp.where(incl, gk[n:], 0.0)
        w_h = _dot_hi(m_ak, v_h)
        xa = a_h + _dot_hi(m_ab, a_h)
        xw = w_h + _dot_hi(m_ab, w_h)
        p = m_ab
        for _ in range(5):
            p = _dot_hi(p, p)
            xa = xa + _dot_hi(p, xa)
            xw = xw + _dot_hi(p, xw)
        s = s_scr[h]
        u = _dot_nt_hi(xa, s) + xw
        y_ref[:, sl] = _dot_nt_hi(r_h, s) + _dot_hi(g_rb, u) + _dot_hi(g_rk, v_h)
        s_scr[h] = s * p_all[:, sl] + _dot_tn_hi(u, bb[:, sl]) + _dot_tn_hi(v_h, kb[:, sl])

    @pl.when(c == pl.num_programs(2) - 1)
    def _():
        sf_ref[...] = s_scr[...]


def _rwkv_scan(r, v, kk, e, km, bvec, s0):
    bx, tx, _ = r.shape
    nc = tx // CHUNK
    pos = lambda d, c: c + d * (nc - 1 - 2 * c)
    one = pl.BlockSpec((None, CHUNK, RW_WIDTH), lambda d, b, c: (b, pos(d, c), 0))
    two = pl.BlockSpec((None, None, CHUNK, RW_WIDTH), lambda d, b, c: (d, b, pos(d, c), 0))
    st = pl.BlockSpec((None, None, RW_HEADS, RW_HEAD, RW_HEAD), lambda d, b, c: (d, b, 0, 0, 0))
    return pl.pallas_call(
        _rw_scan_body,
        grid=(2, bx, nc),
        in_specs=[one, one, one, two, two, two, st],
        out_specs=[two, st],
        out_shape=[jax.ShapeDtypeStruct((2, bx, tx, RW_WIDTH), F32),
                   jax.ShapeDtypeStruct((2, bx, RW_HEADS, RW_HEAD, RW_HEAD), F32)],
        scratch_shapes=[pltpu.VMEM((RW_HEADS, RW_HEAD, RW_HEAD), F32)],
        compiler_params=_cparams(("arbitrary", "arbitrary", "arbitrary")),
        name="rwkv_scan",
    )(r, v, kk, e, km, bvec, s0)


def _merge_body(hy_ref, y_ref, bonus_ref, g_ref, gate_ref, x_ref, hyp_ref, rwp_ref, wo_ref, ones_ref,
                lng_ref, lnb_ref, m2_ref, n1_ref, n2_ref, sc4_ref, m3_ref, rw_ref,
                x1_ref, h2_ref, aff_ref):
    ones = ones_ref[...]
    ysum = y_ref[0] + y_ref[1]
    mean = _dot_hi(ysum, ones) * (1.0 / RW_HEAD)
    dev = ysum - mean
    var = _dot_hi(dev * dev, ones) * (1.0 / RW_HEAD)
    yn = dev * lax.rsqrt(var + RW_LN_EPS) * lng_ref[...] + lnb_ref[...]
    y_rw = (yn + bonus_ref[...]) * g_ref[...]
    gates = _sigmoid(gate_ref[...])
    mix = gates[:, :D_MODEL] * _dot_bf(hy_ref[...], hyp_ref[...]) + gates[:, D_MODEL:] * _dot_bf(y_rw, rwp_ref[...])
    mixed = _dot_bf(mix, wo_ref[...])
    x1 = x_ref[...] + m2_ref[...] * (_rms(mixed) * n1_ref[...])
    x1_ref[...] = x1
    h2 = _rms(x1) * n2_ref[...] * sc4_ref[...] + m3_ref[...]
    h2_ref[...] = h2
    logits = _dot_hi(h2, rw_ref[...])
    lane = lax.broadcasted_iota(I32, logits.shape, 1)
    logits = jnp.where(lane < N_EXPERTS, logits, -1e30)
    ex = jnp.exp(logits - jnp.max(logits, axis=1, keepdims=True))
    aff_ref[...] = ex / jnp.sum(ex, axis=1, keepdims=True)


def _merge(out_hy, y, bonus, g, gate_cols, x, hy_proj, rw_proj, w_out, ln_g, ln_b, m2, n1post, n2pre, sc4, m3,
           router_w):
    tt = 256
    tok = lambda w: pl.BlockSpec((None, tt, w), lambda b, t: (b, t, 0))
    full = lambda a: pl.BlockSpec(a.shape, lambda b, t: (0,) * a.ndim)
    per_b = pl.BlockSpec((None, 1, D_MODEL), lambda b, t: (b, 0, 0))
    vec = lambda a, n: a.reshape(1, n)
    consts = (hy_proj.astype(BF16), rw_proj.astype(BF16), w_out.astype(BF16), _head_ones(),
              vec(ln_g, RW_WIDTH), vec(ln_b, RW_WIDTH))
    router128 = _pad2(router_w, D_MODEL, 128)
    return pl.pallas_call(
        _merge_body,
        grid=(BATCH, SEQ // tt),
        in_specs=[tok(HY_WIDTH),
                  pl.BlockSpec((2, None, tt, RW_WIDTH), lambda b, t: (0, b, t, 0)),
                  tok(RW_WIDTH), tok(RW_WIDTH), tok(GATE_COLS), tok(D_MODEL)]
                 + [full(a) for a in consts]
                 + [per_b, full(vec(n1post, D_MODEL)), full(vec(n2pre, D_MODEL)), per_b, per_b,
                    full(router128)],
        out_specs=[tok(D_MODEL), tok(D_MODEL), tok(128)],
        out_shape=[jax.ShapeDtypeStruct((BATCH, SEQ, D_MODEL), F32),
                   jax.ShapeDtypeStruct((BATCH, SEQ, D_MODEL), F32),
                   jax.ShapeDtypeStruct((BATCH, SEQ, 128), F32)],
        compiler_params=_cparams(("arbitrary", "arbitrary"), VMEM_LIMIT),
        name="merge_router",
    )(out_hy, y, bonus, g, gate_cols, x, *consts, m2, vec(n1post, D_MODEL), vec(n2pre, D_MODEL), sc4, m3,
      router128)


def _topk_body(aff_ref, idx_ref, gate_ref, cnt_scr, key_scr):
    def bisect(_, carry):
        lo, hi = carry
        mid = lo + ((hi - lo) >> 1)
        n_ge = jnp.sum(jnp.where(aff_ref[...] >= pltpu.bitcast(mid, F32), 1.0, 0.0), axis=0, keepdims=True)
        ok = n_ge >= float(CAP)
        return jnp.where(ok, mid, lo), jnp.where(ok, hi, mid)

    lo0 = jnp.zeros((1, 128), I32)
    hi0 = jnp.full((1, 128), 0x7F800000, I32)
    thr_bits, _ = lax.fori_loop(0, 32, bisect, (lo0, hi0))
    thr = pltpu.bitcast(thr_bits, F32)
    need = float(CAP) - jnp.sum(jnp.where(aff_ref[...] > thr, 1.0, 0.0), axis=0, keepdims=True)

    ri = lax.broadcasted_iota(I32, (128, 128), 0)
    ci = lax.broadcasted_iota(I32, (128, 128), 1)
    lower = jnp.where(ri >= ci, 1.0, 0.0).astype(BF16)
    off_eq = jnp.zeros((1, 128), F32)
    off_sel = jnp.zeros((1, 128), F32)
    for blk in range(SEQ // 128):
        sl = slice(128 * blk, 128 * (blk + 1))
        aff_b = aff_ref[sl, :]
        eq_b = jnp.where(aff_b == thr, 1.0, 0.0)
        inc = jnp.dot(lower, eq_b.astype(BF16), preferred_element_type=F32) + off_eq
        off_eq = inc[127:128, :]
        sel_b = jnp.where(aff_b > thr, 1.0, 0.0) + eq_b * jnp.where(inc - eq_b < need, 1.0, 0.0)
        cnt = jnp.dot(lower, sel_b.astype(BF16), preferred_element_type=F32) + off_sel
        off_sel = cnt[127:128, :]
        cnt_scr[sl, :] = cnt
        key_scr[sl, :] = cnt * sel_b

    slot_lane = lax.broadcasted_iota(I32, (128, 128), 1).astype(F32)
    for ex in range(N_EXPERTS):
        col = slice(ex, ex + 1)

        def per_slots(sc, carry, col=col, ex=ex):
            slot = slot_lane + (sc * 128).astype(F32)

            def per_tokens(tb, acc):
                acc_i, acc_g = acc
                rows = pl.ds(pl.multiple_of(tb * 128, 128), 128)
                acc_i = acc_i + jnp.where(cnt_scr[rows, col] <= slot, 1.0, 0.0)
                acc_g = acc_g + jnp.where(key_scr[rows, col] == slot + 1.0, aff_ref[rows, col], 0.0)
                return acc_i, acc_g

            z = jnp.zeros((128, 128), F32)
            acc_i, acc_g = lax.fori_loop(0, SEQ // 128, per_tokens, (z, z))
            lanes = pl.ds(pl.multiple_of(sc * 128, 128), 128)
            idx_ref[col, lanes] = jnp.sum(acc_i, axis=0, keepdims=True).astype(I32)
            gate_ref[col, lanes] = jnp.sum(acc_g, axis=0, keepdims=True)
            return carry

        lax.fori_loop(0, CAP // 128, per_slots, 0)


def _expert_choice(aff):
    return pl.pallas_call(
        _topk_body,
        grid=(BATCH,),
        in_specs=[pl.BlockSpec((None, SEQ, 128), lambda b: (b, 0, 0))],
        out_specs=[pl.BlockSpec((None, N_EXPERTS, CAP), lambda b: (b, 0, 0))] * 2,
        out_shape=[jax.ShapeDtypeStruct((BATCH, N_EXPERTS, CAP), I32),
                   jax.ShapeDtypeStruct((BATCH, N_EXPERTS, CAP), F32)],
        scratch_shapes=[pltpu.VMEM((SEQ, 128), F32)] * 2,
        compiler_params=_cparams(("arbitrary",), VMEM_LIMIT),
        name="expert_choice",
    )(aff)


HALF = D_MODEL // 2


def _gather_body(idx_ref, h_ref, o_ref, buf):
    sub = lax.broadcasted_iota(I32, (8, HALF), 0)

    def group(gi, carry):
        base = pl.multiple_of(gi * 8, 8)
        acc = jnp.zeros((8, HALF), F32)
        for j in range(8):
            row = idx_ref[0, base + j]
            tile = h_ref[pl.ds(pl.multiple_of((row >> 3) << 3, 8), 8), :]
            acc = jnp.where(sub == j, pltpu.roll(tile, (j - row) & 7, 0), acc)
        buf[pl.ds(base, 8), :] = acc
        return carry

    lax.fori_loop(0, CAP // 8, group, 0)
    o_ref[...] = buf[...].astype(BF16)


def _gather(idx3, h2):
    return pl.pallas_call(
        _gather_body,
        grid=(BATCH, 2, N_EXPERTS),
        in_specs=[pl.BlockSpec((None, 1, CAP), lambda b, dh, e: (b * N_EXPERTS + e, 0, 0),
                               memory_space=pltpu.SMEM),
                  pl.BlockSpec((None, SEQ, HALF), lambda b, dh, e: (b, 0, dh))],
        out_specs=pl.BlockSpec((None, None, CAP, HALF), lambda b, dh, e: (b, e, 0, dh)),
        out_shape=jax.ShapeDtypeStruct((BATCH, N_EXPERTS, CAP, D_MODEL), BF16),
        scratch_shapes=[pltpu.VMEM((CAP, HALF), F32)],
        compiler_params=_cparams(("arbitrary", "arbitrary", "arbitrary"), VMEM_LIMIT),
        name="moe_gather",
    )(idx3, h2)


def _ffn_body(x_ref, w1_ref, w3_ref, w2_ref, gt_ref, o_ref):
    f = pl.program_id(2)
    x = x_ref[...]
    a = jnp.dot(x, w1_ref[...].astype(BF16), preferred_element_type=F32)
    b = jnp.dot(x, w3_ref[...].astype(BF16), preferred_element_type=F32)
    hid = (a * _sigmoid(a) * b).astype(BF16)
    part = jnp.dot(hid, w2_ref[...].astype(BF16), preferred_element_type=F32)

    @pl.when(f == 0)
    def _():
        o_ref[...] = part

    @pl.when(f > 0)
    def _():
        o_ref[...] += part

    @pl.when(f == pl.num_programs(2) - 1)
    def _():
        o_ref[...] = o_ref[...] * gt_ref[...]


def _experts(xe, w1, w3, w2, gates4):
    tf = 512
    return pl.pallas_call(
        _ffn_body,
        grid=(BATCH, N_EXPERTS, D_EXPERT // tf),
        in_specs=[pl.BlockSpec((None, None, CAP, D_MODEL), lambda b, e, f: (b, e, 0, 0)),
                  pl.BlockSpec((None, D_MODEL, tf), lambda b, e, f: (e, 0, f)),
                  pl.BlockSpec((None, D_MODEL, tf), lambda b, e, f: (e, 0, f)),
                  pl.BlockSpec((None, tf, D_MODEL), lambda b, e, f: (e, f, 0)),
                  pl.BlockSpec((None, None, CAP, 1), lambda b, e, f: (b, e, 0, 0))],
        out_specs=pl.BlockSpec((None, None, CAP, D_MODEL), lambda b, e, f: (b, e, 0, 0)),
        out_shape=jax.ShapeDtypeStruct((BATCH, N_EXPERTS, CAP, D_MODEL), F32),
        compiler_params=_cparams(("arbitrary", "arbitrary", "arbitrary"), VMEM_LIMIT),
        name="moe_experts",
    )(xe, w1, w3, w2, gates4)


def _scatter_body(idx_ref, ye_ref, o_ref):
    @pl.when(pl.program_id(2) == 0)
    def _():
        o_ref[...] = jnp.zeros_like(o_ref)

    sub = lax.broadcasted_iota(I32, (8, HALF), 0)

    def group(gi, carry):
        base = pl.multiple_of(gi * 8, 8)
        rows8 = ye_ref[pl.ds(base, 8), :]
        for j in range(8):
            row = idx_ref[0, base + j]
            tile = pl.ds(pl.multiple_of((row >> 3) << 3, 8), 8)
            add = jnp.where(sub == (row & 7), jnp.broadcast_to(rows8[j:j + 1, :], (8, HALF)), 0.0)
            o_ref[tile, :] = o_ref[tile, :] + add
        return carry

    lax.fori_loop(0, CAP // 8, group, 0)


def _scatter(idx3, ye):
    return pl.pallas_call(
        _scatter_body,
        grid=(BATCH, 2, N_EXPERTS),
        in_specs=[pl.BlockSpec((None, 1, CAP), lambda b, dh, e: (b * N_EXPERTS + e, 0, 0),
                               memory_space=pltpu.SMEM),
                  pl.BlockSpec((None, None, CAP, HALF), lambda b, dh, e: (b, e, 0, dh))],
        out_specs=pl.BlockSpec((None, SEQ, HALF), lambda b, dh, e: (b, 0, dh)),
        out_shape=jax.ShapeDtypeStruct((BATCH, SEQ, D_MODEL), F32),
        compiler_params=_cparams(("arbitrary", "arbitrary", "arbitrary"), VMEM_LIMIT),
        name="moe_scatter",
    )(idx3, ye)


def _final_body(x1_ref, moe_ref, m5_ref, n_ref, o_ref):
    o_ref[...] = x1_ref[...] + m5_ref[...] * (_rms(moe_ref[...]) * n_ref[...])


def _final(x1, moe, m5, n2post):
    tt = 512
    tok = pl.BlockSpec((None, tt, D_MODEL), lambda b, t: (b, t, 0))
    return pl.pallas_call(
        _final_body,
        grid=(BATCH, SEQ // tt),
        in_specs=[tok, tok, pl.BlockSpec((None, 1, D_MODEL), lambda b, t: (b, 0, 0)),
                  pl.BlockSpec((1, D_MODEL), lambda b, t: (0, 0))],
        out_specs=tok,
        out_shape=jax.ShapeDtypeStruct((BATCH, SEQ, D_MODEL), F32),
        compiler_params=_cparams(("arbitrary", "arbitrary")),
        name="moe_residual",
    )(x1, moe, m5, n2post.reshape(1, D_MODEL))


def _layer(x, c, ctx, c_ctx, p):
    c8 = jnp.zeros((8, D_MODEL), F32).at[:BATCH].set(c).at[BATCH].set(c_ctx)
    mod = _modulation(c8, p['mod_w'], p['mod_b'])
    m_lat = [mod[:BATCH, i * D_MODEL:(i + 1) * D_MODEL].reshape(BATCH, 1, D_MODEL) for i in range(6)]
    m_ctx = [jnp.broadcast_to(mod[BATCH, i * D_MODEL:(i + 1) * D_MODEL], (BATCH, 1, D_MODEL)) for i in range(2)]

    w_in = p['w_in'].astype(BF16)
    w_hy = w_in[:, :HY_COLS]
    w_rw = w_in[:, HY_COLS:HY_COLS + RW_COLS]
    w_gate = w_in[:, HY_COLS + RW_COLS:]
    rw_args = (p['rw_mu'], p['rw_w_up'], p['rw_a_up'], p['rw_w0'], p['rw_a0'], p['rw_g_up'],
               p['rw_k_k'], p['rw_k_a'], p['rw_r_k'])

    (ctx_cols,) = _in_projection(ctx, p['norm1_pre'], 1.0 + m_ctx[1], m_ctx[0], [w_rw], CTX_LEN, 1)
    cr, cv, ckk, ce, ckm, cb, _, _ = _rwkv_prep(ctx_cols, *rw_args)
    zero_state = jnp.zeros((2, BATCH, RW_HEADS, RW_HEAD, RW_HEAD), F32)
    _, ctx_states = _rwkv_scan(cr, cv, ckk, ce, ckm, cb, zero_state)

    hy_cols, rw_cols, gate_cols = _in_projection(x, p['norm1_pre'], 1.0 + m_lat[1], m_lat[0],
                                                 [w_hy, w_rw, w_gate], 512, 2)
    out_hy = _hyena_branch(hy_cols, p['hy_conv_w'], p['hy_conv_b'], p['hy_ffn_w1'], p['hy_ffn_b1'],
                           p['hy_ffn_w2'], p['hy_ffn_b2'], p['hy_ffn_w3'], p['hy_freq'], p['hy_bias'])
    r, v, kk, e, km, bvec, bonus, g = _rwkv_prep(rw_cols, *rw_args)
    y, _ = _rwkv_scan(r, v, kk, e, km, bvec, ctx_states)

    x1, h2, aff = _merge(out_hy, y, bonus, g, gate_cols, x, p['hy_proj'], p['rw_proj'], p['w_out'],
                           p['rw_ln_g'], p['rw_ln_b'], m_lat[2], p['norm1_post'], p['norm2_pre'],
                           1.0 + m_lat[4], m_lat[3], p['router_w'])
    idx, gates = _expert_choice(aff)
    idx3 = idx.reshape(BATCH * N_EXPERTS, 1, CAP)
    gates4 = gates.reshape(BATCH, N_EXPERTS, CAP, 1)
    xe = _gather(idx3, h2)
    ye = _experts(xe, p['exp_w1'], p['exp_w3'], p['exp_w2'], gates4)
    moe = _scatter(idx3, ye)
    return _final(x1, moe, m_lat[5], p['norm2_post'])


def kernel(x, c, ctx, c_ctx, mod_w, mod_b, norm1_pre, norm1_post, norm2_pre, norm2_post, w_in, hy_conv_w, hy_conv_b, hy_ffn_w1, hy_ffn_b1, hy_ffn_w2, hy_ffn_b2, hy_ffn_w3, hy_freq, hy_bias, hy_proj, rw_mu, rw_w0, rw_w_up, rw_a0, rw_a_up, rw_g_up, rw_k_k, rw_k_a, rw_r_k, rw_ln_g, rw_ln_b, rw_proj, w_out, router_w, exp_w1, exp_w3, exp_w2):
    names = ('mod_w', 'mod_b', 'norm1_pre', 'norm1_post', 'norm2_pre', 'norm2_post', 'w_in', 'hy_conv_w',
             'hy_conv_b', 'hy_ffn_w1', 'hy_ffn_b1', 'hy_ffn_w2', 'hy_ffn_b2', 'hy_ffn_w3', 'hy_freq', 'hy_bias',
             'hy_proj', 'rw_mu', 'rw_w0', 'rw_w_up', 'rw_a0', 'rw_a_up', 'rw_g_up', 'rw_k_k', 'rw_k_a', 'rw_r_k',
             'rw_ln_g', 'rw_ln_b', 'rw_proj', 'w_out', 'router_w', 'exp_w1', 'exp_w3', 'exp_w2')
    vals = (mod_w, mod_b, norm1_pre, norm1_post, norm2_pre, norm2_post, w_in, hy_conv_w, hy_conv_b, hy_ffn_w1,
            hy_ffn_b1, hy_ffn_w2, hy_ffn_b2, hy_ffn_w3, hy_freq, hy_bias, hy_proj, rw_mu, rw_w0, rw_w_up, rw_a0,
            rw_a_up, rw_g_up, rw_k_k, rw_k_a, rw_r_k, rw_ln_g, rw_ln_b, rw_proj, w_out, router_w, exp_w1, exp_w3,
            exp_w2)
    depth = mod_w.shape[0]
    assert depth == 1, "single-layer block: the context stream only provides scan start states"
    p = {n: a[0] for n, a in zip(names, vals)}
    return _layer(x, c, ctx, c_ctx, p)
```

```python
import functools
import math

import numpy as np
import jax
import jax.numpy as jnp
from jax import lax
from jax.experimental import pallas as pl
from jax.experimental.pallas import tpu as pltpu

F32 = jnp.float32
BF16 = jnp.bfloat16
I32 = jnp.int32
HI = lax.Precision.HIGHEST

D_MODEL = 1024
BATCH = 4
SEQ = 8192
CTX_LEN = 256

HY_WIDTH = 512
HY_BANDS = 16
HY_FILTER_ORDER = 64
HY_DECAY_TARGET = 1e-2
HY_DECAY_SHORT = 0.3
HY_DECAY_LONG = 1.5

RW_HEADS = 8
RW_HEAD = 64
RW_WIDTH = RW_HEADS * RW_HEAD
LORA_W = 64
LORA_A = 64
LORA_G = 128
RW_LN_EPS = 64e-5

HY_COLS = 3 * HY_WIDTH
RW_COLS = 3 * RW_WIDTH + LORA_W + LORA_A + LORA_G
GATE_COLS = 2 * D_MODEL

N_EXPERTS = 16
D_EXPERT = 2048
EC_CAPACITY = 2
NORM_EPS = 1e-6
CAP = EC_CAPACITY * SEQ // N_EXPERTS

NFFT = 2 * SEQ
FFT_R = 128
FFT_H = FFT_R // 2
FFT_J = 8
CHUNK = 64

VMEM_LIMIT = 56 * 1024 * 1024


def _cparams(sem, vmem=None):
    return pltpu.CompilerParams(dimension_semantics=sem, vmem_limit_bytes=vmem)


def _dot_hi(a, b):
    return jnp.dot(a, b, precision=HI, preferred_element_type=F32)


def _dot_nt_hi(a, b):
    return lax.dot_general(a, b, (((1,), (1,)), ((), ())), precision=HI, preferred_element_type=F32)


def _dot_tn_hi(a, b):
    return lax.dot_general(a, b, (((0,), (0,)), ((), ())), precision=HI, preferred_element_type=F32)


def _dot_bf(a, b):
    return jnp.dot(a.astype(BF16), b.astype(BF16), preferred_element_type=F32)


def _rms(x):
    return x * lax.rsqrt(jnp.mean(x * x, axis=-1, keepdims=True) + NORM_EPS)


def _sigmoid(x):
    return 1.0 / (1.0 + jnp.exp(-x))


def _mod_body(c_ref, w_ref, b_ref, o_ref):
    c = c_ref[...]
    o_ref[...] = _dot_hi(c * _sigmoid(c), w_ref[...]) + b_ref[...]


def _modulation(c8, mod_w, mod_b):
    n = mod_w.shape[1]
    tn = 1536
    return pl.pallas_call(
        _mod_body,
        grid=(n // tn,),
        in_specs=[pl.BlockSpec((8, D_MODEL), lambda j: (0, 0)),
                  pl.BlockSpec((D_MODEL, tn), lambda j: (0, j)),
                  pl.BlockSpec((1, tn), lambda j: (0, j))],
        out_specs=pl.BlockSpec((8, tn), lambda j: (0, j)),
        out_shape=jax.ShapeDtypeStruct((8, n), F32),
        compiler_params=_cparams(("arbitrary",)),
        name="modulation",
    )(c8, mod_w, mod_b.reshape(1, n))


def _inproj_body(nw, x_ref, g_ref, sc_ref, sh_ref, *refs):
    w_refs, o_refs, h_ref = refs[:nw], refs[nw:2 * nw], refs[2 * nw]

    @pl.when(pl.program_id(2) == 0)
    def _():
        h = _rms(x_ref[...]) * g_ref[...] * sc_ref[...] + sh_ref[...]
        h_ref[...] = h.astype(BF16)

    h = h_ref[...]
    for w_ref, o_ref in zip(w_refs, o_refs):
        o_ref[...] = jnp.dot(h, w_ref[...], preferred_element_type=F32)


def _in_projection(x, g, scale1p, shift, weights, tm, nsplit):
    bx, tx, _ = x.shape
    nw = len(weights)
    tns = [w.shape[1] // nsplit for w in weights]
    in_specs = [pl.BlockSpec((None, tm, D_MODEL), lambda b, m, n: (b, m, 0)),
                pl.BlockSpec((1, D_MODEL), lambda b, m, n: (0, 0)),
                pl.BlockSpec((None, 1, D_MODEL), lambda b, m, n: (b, 0, 0)),
                pl.BlockSpec((None, 1, D_MODEL), lambda b, m, n: (b, 0, 0))]
    in_specs += [pl.BlockSpec((D_MODEL, tn), lambda b, m, n: (0, n)) for tn in tns]
    out_specs = [pl.BlockSpec((None, tm, tn), lambda b, m, n: (b, m, n)) for tn in tns]
    out_shape = [jax.ShapeDtypeStruct((bx, tx, w.shape[1]), F32) for w in weights]
    return pl.pallas_call(
        functools.partial(_inproj_body, nw),
        grid=(bx, tx // tm, nsplit),
        in_specs=in_specs,
        out_specs=out_specs,
        out_shape=out_shape,
        scratch_shapes=[pltpu.VMEM((tm, D_MODEL), BF16)],
        compiler_params=_cparams(("arbitrary", "arbitrary", "arbitrary"), VMEM_LIMIT),
        name="in_projection",
    )(x, g.reshape(1, D_MODEL), scale1p, shift, *weights)


def _neighbours(cur, pv_ref, nx_ref):
    tt = cur.shape[0]
    t = pl.program_id(1)
    first = (t > 0).astype(F32)
    last = (t < pl.num_programs(1) - 1).astype(F32)
    prev_row = pv_ref[7:8, :] * first
    next_row = nx_ref[0:1, :] * last
    rows = lax.broadcasted_iota(I32, cur.shape, 0)
    prev = jnp.where(rows == 0, prev_row, pltpu.roll(cur, 1, 0))
    nxt = jnp.where(rows == tt - 1, next_row, pltpu.roll(cur, tt - 1, 0))
    return prev, nxt


def _halo_specs(tt, t_total, width):
    r8 = tt // 8
    last8 = t_total // 8 - 1
    return [pl.BlockSpec((None, tt, width), lambda b, t: (b, t, 0)),
            pl.BlockSpec((None, 8, width), lambda b, t: (b, jnp.maximum(t * r8 - 1, 0), 0)),
            pl.BlockSpec((None, 8, width), lambda b, t: (b, jnp.minimum((t + 1) * r8, last8), 0))]


def _hy_pre_body(cur_ref, pv_ref, nx_ref, w_ref, b_ref, z_ref, x0_ref):
    cur = cur_ref[...]
    prev, nxt = _neighbours(cur, pv_ref, nx_ref)
    u = w_ref[0:1, :] * prev + w_ref[1:2, :] * cur + w_ref[2:3, :] * nxt + b_ref[...]
    z_ref[...] = u[:, 2 * HY_WIDTH:] * u[:, HY_WIDTH:2 * HY_WIDTH]
    x0_ref[...] = u[:, :HY_WIDTH]


def _hyena_pre(hy_cols, conv_w, conv_b):
    tt = 512
    spec_o = pl.BlockSpec((None, tt, HY_WIDTH), lambda b, t: (b, t, 0))
    return pl.pallas_call(
        _hy_pre_body,
        grid=(BATCH, SEQ // tt),
        in_specs=_halo_specs(tt, SEQ, HY_COLS) + [
            pl.BlockSpec((3, HY_COLS), lambda b, t: (0, 0)),
            pl.BlockSpec((1, HY_COLS), lambda b, t: (0, 0))],
        out_specs=[spec_o, spec_o],
        out_shape=[jax.ShapeDtypeStruct((BATCH, SEQ, HY_WIDTH), F32)] * 2,
        compiler_params=_cparams(("arbitrary", "arbitrary"), VMEM_LIMIT),
        name="hyena_pre",
    )(hy_cols, hy_cols, hy_cols, conv_w, conv_b.reshape(1, HY_COLS))


def _filt_body(bands_ref, w1t_ref, w1c_ref, w1s_ref, b1_ref, w2_ref, b2_ref, w3_ref, fr_ref, dl_ref,
               f_ref, s_ref):
    i = pl.program_id(0)
    tt = f_ref.shape[0]
    rowi = lax.broadcasted_iota(I32, (tt, 1), 0) + i * tt
    rowf = rowi.astype(F32)
    tpos = rowf * (1.0 / (SEQ - 1))
    arg = (rowf * (2.0 * math.pi / SEQ)) * bands_ref[...]
    pre = tpos * w1t_ref[...] + _dot_hi(jnp.cos(arg), w1c_ref[...]) - _dot_hi(jnp.sin(arg), w1s_ref[...]) + b1_ref[...]
    freq = fr_ref[...]
    hid = jnp.sin(freq * pre)
    hid = jnp.sin(freq * (_dot_hi(hid, w2_ref[...]) + b2_ref[...]))
    filt = _dot_hi(hid, w3_ref[...]) * jnp.exp(-tpos * dl_ref[...])
    lane = lax.broadcasted_iota(I32, filt.shape, 1)
    rows = lax.broadcasted_iota(I32, filt.shape, 0) + i * tt
    filt = jnp.where((rows == 0) & (lane >= HY_WIDTH), 0.0, filt)
    f_ref[...] = filt

    @pl.when(i == 0)
    def _():
        s_ref[...] = jnp.zeros_like(s_ref)

    s_ref[...] += jnp.sum(jnp.abs(filt), axis=0, keepdims=True)


def _pad2(a, rows, cols):
    return jnp.zeros((rows, cols), F32).at[:a.shape[0], :a.shape[1]].set(a)


def _hyena_filter(w1, b1, w2, b2, w3, freq):
    tt = 512
    fo = HY_FILTER_ORDER
    bands = np.zeros((1, 128), np.float32)
    bands[0, :HY_BANDS] = np.linspace(1e-4, HY_BANDS - 1, HY_BANDS, dtype=np.float32)
    deltas = np.abs(np.linspace(math.log(HY_DECAY_TARGET) / HY_DECAY_LONG,
                                math.log(HY_DECAY_TARGET) / HY_DECAY_SHORT, HY_WIDTH, dtype=np.float32))
    dl = np.concatenate([deltas, deltas])[None, :].astype(np.float32)
    args = (jnp.asarray(bands),
            _pad2(w1[0:1], 1, 128),
            _pad2(w1[1:1 + HY_BANDS], 128, 128),
            _pad2(w1[1 + HY_BANDS:], 128, 128),
            _pad2(b1[None], 1, 128),
            _pad2(w2, 128, 128),
            _pad2(b2[None], 1, 128),
            _pad2(w3, 128, 2 * HY_WIDTH),
            _pad2(freq[None], 1, 128),
            jnp.asarray(dl))
    del fo
    full = lambda a: pl.BlockSpec(a.shape, lambda i: (0, 0))
    return pl.pallas_call(
        _filt_body,
        grid=(SEQ // tt,),
        in_specs=[full(a) for a in args],
        out_specs=[pl.BlockSpec((tt, 2 * HY_WIDTH), lambda i: (i, 0)),
                   pl.BlockSpec((1, 2 * HY_WIDTH), lambda i: (0, 0))],
        out_shape=[jax.ShapeDtypeStruct((SEQ, 2 * HY_WIDTH), F32),
                   jax.ShapeDtypeStruct((1, 2 * HY_WIDTH), F32)],
        compiler_params=_cparams(("arbitrary",)),
        name="hyena_filter",
    )(*args)


def _fft_tables():
    k = np.arange(FFT_R)
    n = np.arange(FFT_H)
    th = 2.0 * np.pi * np.outer(k, n) / FFT_R
    f1 = np.stack([np.cos(th), -np.sin(th)])
    eye = np.eye(FFT_J)
    g1 = np.einsum('pkn,jm->pkjnm', f1, eye).reshape(2 * FFT_R * FFT_J, FFT_H * FFT_J)
    f3 = np.stack([np.cos(th.T), -np.sin(th.T)]) / NFFT
    g3 = np.einsum('pnk,jm->njpkm', f3, eye).reshape(FFT_H * FFT_J, 2 * FFT_R * FFT_J)
    th2 = 2.0 * np.pi * np.outer(k, k) / FFT_R
    c, s = np.cos(th2), np.sin(th2)
    fc = np.block([[c, s], [-s, c]])
    fi = np.block([[c, -s], [s, c]])
    return tuple(jnp.asarray(a, F32) for a in (g1, g3, fc, fi))


def _twiddle(rows_k, n2):
    ph = ((rows_k * n2) & (NFFT - 1)).astype(F32) * (2.0 * math.pi / NFFT)
    return jnp.cos(ph), jnp.sin(ph)


def _fft1_body(z_ref, g1_ref, o_ref):
    nb = pl.program_id(1)
    c = z_ref.shape[-1]
    z = z_ref[...].reshape(FFT_H * FFT_J, c)
    a = _dot_hi(g1_ref[...], z)
    half = FFT_R * FFT_J
    ar, ai = a[:half], a[half:]
    r = lax.broadcasted_iota(I32, (half, 1), 0)
    cs, sn = _twiddle(r >> 3, nb * FFT_J + (r & (FFT_J - 1)))
    o_ref[0] = (ar * cs + ai * sn).reshape(FFT_R, FFT_J, c)
    o_ref[1] = (ai * cs - ar * sn).reshape(FFT_R, FFT_J, c)


def _fft_stage1(z4, g1):
    bz, _, _, c = z4.shape
    return pl.pallas_call(
        _fft1_body,
        grid=(bz, FFT_R // FFT_J),
        in_specs=[pl.BlockSpec((None, FFT_H, FFT_J, c), lambda b, n: (b, 0, n, 0)),
                  pl.BlockSpec(g1.shape, lambda b, n: (0, 0))],
        out_specs=pl.BlockSpec((None, 2, FFT_R, FFT_J, c), lambda b, n: (b, 0, 0, n, 0)),
        out_shape=jax.ShapeDtypeStruct((bz, 2, FFT_R, FFT_R, c), F32),
        compiler_params=_cparams(("arbitrary", "arbitrary"), VMEM_LIMIT),
        name="fft_stage1",
    )(z4, g1)


def _filt_spec_body(a_ref, fc_ref, h_ref):
    x = _dot_hi(fc_ref[...], jnp.concatenate([a_ref[0], a_ref[1]], axis=0))
    xr, xi = x[:FFT_R], x[FFT_R:]
    h_ref[0] = xr[:, :HY_WIDTH] + xr[:, HY_WIDTH:]
    h_ref[1] = xi[:, :HY_WIDTH] - xi[:, HY_WIDTH:]


def _filter_spectrum(a, fc):
    return pl.pallas_call(
        _filt_spec_body,
        grid=(FFT_R,),
        in_specs=[pl.BlockSpec((None, 2, None, FFT_R, 2 * HY_WIDTH), lambda k: (0, 0, k, 0, 0)),
                  pl.BlockSpec(fc.shape, lambda k: (0, 0))],
        out_specs=pl.BlockSpec((None, 2, FFT_R, HY_WIDTH), lambda k: (k, 0, 0, 0)),
        out_shape=jax.ShapeDtypeStruct((FFT_R, 2, FFT_R, HY_WIDTH), F32),
        compiler_params=_cparams(("arbitrary",)),
        name="filter_spectrum",
    )(a, fc)


def _fft_mid_body(a_ref, h_ref, fc_ref, fi_ref, o_ref):
    k1 = pl.program_id(0)
    x = _dot_hi(fc_ref[...], jnp.concatenate([a_ref[0], a_ref[1]], axis=0))
    xr, xi = x[:FFT_R], x[FFT_R:]
    hr, hi = h_ref[0], h_ref[1]
    yr = xr * hr - xi * hi
    yi = xr * hi + xi * hr
    p = _dot_hi(fi_ref[...], jnp.concatenate([yr, yi], axis=0))
    pr, pim = p[:FFT_R], p[FFT_R:]
    cs, sn = _twiddle(lax.broadcasted_iota(I32, (FFT_R, 1), 0), k1)
    o_ref[0] = pr * cs - pim * sn
    o_ref[1] = pr * sn + pim * cs


def _fft_mid(a, h, fc, fi):
    bz = a.shape[0]
    blk = pl.BlockSpec((None, 2, None, FFT_R, HY_WIDTH), lambda k, b: (b, 0, k, 0, 0))
    return pl.pallas_call(
        _fft_mid_body,
        grid=(FFT_R, bz),
        in_specs=[blk,
                  pl.BlockSpec((None, 2, FFT_R, HY_WIDTH), lambda k, b: (k, 0, 0, 0)),
                  pl.BlockSpec(fc.shape, lambda k, b: (0, 0)),
                  pl.BlockSpec(fi.shape, lambda k, b: (0, 0))],
        out_specs=blk,
        out_shape=jax.ShapeDtypeStruct(a.shape, F32),
        compiler_params=_cparams(("arbitrary", "arbitrary")),
        name="fft_mid",
    )(a, h, fc, fi)


def _ifft_body(q_ref, g3_ref, z_ref, x0_ref, inv_ref, hb_ref, o_ref):
    c = HY_WIDTH
    q = q_ref[...].reshape(2 * FFT_R * FFT_J, c)
    conv = _dot_hi(g3_ref[...], q) * inv_ref[...]
    z = z_ref[...].reshape(FFT_H * FFT_J, c)
    x0 = x0_ref[...].reshape(FFT_H * FFT_J, c)
    o_ref[...] = ((conv + hb_ref[...] * z) * x0).reshape(FFT_H, FFT_J, c)


def _ifft_gate(q, g3, z4, x04, inv_norm, hy_bias):
    row = pl.BlockSpec((None, FFT_H, FFT_J, HY_WIDTH), lambda b, n: (b, 0, n, 0))
    vec = pl.BlockSpec((1, HY_WIDTH), lambda b, n: (0, 0))
    return pl.pallas_call(
        _ifft_body,
        grid=(BATCH, FFT_R // FFT_J),
        in_specs=[pl.BlockSpec((None, 2, FFT_R, FFT_J, HY_WIDTH), lambda b, n: (b, 0, 0, n, 0)),
                  pl.BlockSpec(g3.shape, lambda b, n: (0, 0)), row, row, vec, vec],
        out_specs=row,
        out_shape=jax.ShapeDtypeStruct(z4.shape, F32),
        compiler_params=_cparams(("arbitrary", "arbitrary"), VMEM_LIMIT),
        name="ifft_gate",
    )(q, g3, z4, x04, inv_norm, hy_bias)


def _hyena_branch(hy_cols, conv_w, conv_b, w1, b1, w2, b2, w3, freq, hy_bias):
    g1, g3, fc, fi = _fft_tables()
    z, x0 = _hyena_pre(hy_cols, conv_w, conv_b)
    filt, l1 = _hyena_filter(w1, b1, w2, b2, w3, freq)
    inv_norm = 1.0 / (l1[:, :HY_WIDTH] + l1[:, HY_WIDTH:])
    fa = _fft_stage1(filt.reshape(1, FFT_H, FFT_R, 2 * HY_WIDTH), g1)
    h = _filter_spectrum(fa, fc)
    z4 = z.reshape(BATCH, FFT_H, FFT_R, HY_WIDTH)
    za = _fft_stage1(z4, g1)
    q = _fft_mid(za, h, fc, fi)
    out = _ifft_gate(q, g3, z4, x0.reshape(z4.shape), inv_norm, hy_bias.reshape(1, HY_WIDTH))
    return out.reshape(BATCH, SEQ, HY_WIDTH)


def _rw_prep_body(cur_ref, pv_ref, nx_ref, mu_ref, w2_ref, w0_ref, a0_ref, gup_ref, kkw_ref, ka_ref, rk_ref,
                  ones_ref, r_o, v_o, kk_o, e_o, km_o, b_o, bonus_o, g_o):
    cur = cur_ref[...]
    prev, nxt = _neighbours(cur, pv_ref, nx_ref)
    z = cur + mu_ref[0:1, :] * (prev - cur) + mu_ref[1:2, :] * (nxt - cur)
    w_ = RW_WIDTH
    r, k, v = z[:, :w_], z[:, w_:2 * w_], z[:, 2 * w_:3 * w_]
    ll = z[:, 3 * w_:3 * w_ + 128]
    lg = z[:, 3 * w_ + 128:]
    lane = lax.broadcasted_iota(I32, ll.shape, 1)
    tl = jnp.where(lane < LORA_W, jnp.tanh(ll), ll)
    ones = ones_ref[...]
    kkr = k * kkw_ref[...]
    kk = kkr * lax.rsqrt(jnp.maximum(_dot_hi(kkr * kkr, ones), 1e-24))
    bonus = jnp.zeros_like(r)
    for d in range(2):
        proj = _dot_hi(tl, w2_ref[d])
        wl = proj[:, :w_] + w0_ref[d:d + 1, :]
        al = proj[:, w_:] + a0_ref[d:d + 1, :]
        nwl = -wl
        w = -(jnp.maximum(nwl, 0.0) + jnp.log(1.0 + jnp.exp(-jnp.abs(nwl)))) - 0.5
        a = _sigmoid(al)
        km = k * (1.0 + (a - 1.0) * ka_ref[...])
        e_o[d] = jnp.exp(w)
        km_o[d] = km
        b_o[d] = kk * a
        bonus = bonus + _dot_hi(r * km * rk_ref[...], ones)
    r_o[...] = r
    v_o[...] = v
    kk_o[...] = kk
    bonus_o[...] = bonus * v
    g_o[...] = _dot_hi(_sigmoid(lg), gup_ref[...])


def _head_ones():
    h = np.arange(RW_WIDTH) // RW_HEAD
    return jnp.asarray((h[:, None] == h[None, :]).astype(np.float32))


def _rwkv_prep(rw_cols, mu, w_up, a_up, w0, a0, g_up, k_k, k_a, r_k):
    bx, tx, _ = rw_cols.shape
    tt = 256
    w2 = jnp.zeros((2, 128, 2 * RW_WIDTH), F32)
    w2 = w2.at[:, :LORA_W, :RW_WIDTH].set(w_up).at[:, LORA_W:, RW_WIDTH:].set(a_up)
    vec = lambda a: a.reshape(1, RW_WIDTH)
    full2 = lambda a: pl.BlockSpec(a.shape, lambda b, t: (0,) * a.ndim)
    args = (mu, w2, w0, a0, g_up, vec(k_k), vec(k_a), vec(r_k), _head_ones())
    one = pl.BlockSpec((None, tt, RW_WIDTH), lambda b, t: (b, t, 0))
    two = pl.BlockSpec((2, None, tt, RW_WIDTH), lambda b, t: (0, b, t, 0))
    s1 = jax.ShapeDtypeStruct((bx, tx, RW_WIDTH), F32)
    s2 = jax.ShapeDtypeStruct((2, bx, tx, RW_WIDTH), F32)
    return pl.pallas_call(
        _rw_prep_body,
        grid=(bx, tx // tt),
        in_specs=_halo_specs(tt, tx, RW_COLS) + [full2(a) for a in args],
        out_specs=[one, one, one, two, two, two, one, one],
        out_shape=[s1, s1, s1, s2, s2, s2, s1, s1],
        compiler_params=_cparams(("arbitrary", "arbitrary"), VMEM_LIMIT),
        name="rwkv_prep",
    )(rw_cols, rw_cols, rw_cols, *args)


QUAD = 4
QW = QUAD * RW_HEAD


def _chunk_cumsum(e, fwd):
    row = lax.broadcasted_iota(I32, e.shape, 0)
    cf = e
    cb = e
    s = 1
    while s < CHUNK:
        cf = cf + jnp.where(row >= s, pltpu.roll(cf, s, 0), 0.0)
        cb = cb + jnp.where(row < CHUNK - s, pltpu.roll(cb, CHUNK - s, 0), 0.0)
        s *= 2
    return jnp.where(fwd, cf, cb)


def _block_diag(x4, lane_head):
    return jnp.concatenate([jnp.where(lane_head == h, x4, 0.0) for h in range(QUAD)], axis=0).astype(BF16)


def _mm(a, b):
    return jnp.dot(a.astype(BF16), b, preferred_element_type=F32)


def _mm_nt(a, b):
    return lax.dot_general(a.astype(BF16), b, (((1,), (1,)), ((), ())), preferred_element_type=F32)


def _rw_scan_body(r_ref, v_ref, kk_ref, e_ref, km_ref, b_ref, s0_ref, y_ref, sf_ref, s_scr):
    d = pl.program_id(0)
    c = pl.program_id(2)

    @pl.when(c == 0)
    def _():
        s_scr[...] = s0_ref[...]

    n = CHUNK
    fwd = d == 0
    e = e_ref[...]
    cum = _chunk_cumsum(e, fwd)
    tot = jnp.sum(e, axis=0, keepdims=True)
    r, v, kk, km, bv = r_ref[...], v_ref[...], kk_ref[...], km_ref[...], b_ref[...]
    grow = jnp.exp(cum)
    at = -kk * jnp.exp(e - cum)
    bt = bv * grow
    kt = km * grow
    rt = r * jnp.exp(-cum)
    tail = jnp.exp(cum - tot)
    bb = bv * tail
    kb = km * tail
    p_all = jnp.exp(-tot)

    row = lax.broadcasted_iota(I32, (n, QW), 0)
    lane = lax.broadcasted_iota(I32, (n, QW), 1)
    col = lane & (RW_HEAD - 1)
    lane_head = lane >> 6
    later = jnp.where(fwd, row, col)
    earlier = jnp.where(fwd, col, row)
    incl = later >= earlier
    strict = later > earlier
    eye = jnp.where(row == col, 1.0, 0.0)
    bd = lambda x4: _block_diag(x4, lane_head)

    for q in range(RW_HEADS // QUAD):
        ql = slice(QW * q, QW * (q + 1))
        a4, v4 = at[:, ql], v[:, ql]
        ar = jnp.concatenate([a4, rt[:, ql]], axis=0)
        bd_v = bd(v4)
        gb = _mm_nt(ar, bd(bt[:, ql]))
        gk = _mm_nt(ar, bd(kt[:, ql]))
        m = jnp.where(strict, gb[:n], 0.0)
        g_rb = jnp.where(incl, gb[n:], 0.0)
        g_rk = jnp.where(incl, gk[n:], 0.0)
        w4 = _mm(jnp.where(strict, gk[:n], 0.0), bd_v)
        t4 = eye + m
        p = _mm(m, bd(m))
        for level in range(1, 6):
            bd_p = bd(p)
            if level < 5:
                both = _mm(jnp.concatenate([p, t4], axis=0), bd_p)
                p = both[:n]
                t4 = t4 + both[n:]
            else:
                t4 = t4 + _mm(t4, bd_p)
        s4 = s_scr[q]
        sa = _mm_nt(ar, bd(s4))
        u4 = _mm(t4, bd(sa[:n] + w4))
        y_ref[:, ql] = sa[n:] + _mm(g_rb, bd(u4)) + _mm(g_rk, bd_v)
        uv = jnp.concatenate([u4, v4], axis=0).astype(BF16)
        bk = jnp.concatenate([bb[:, ql], kb[:, ql]], axis=0).astype(BF16)
        full = lax.dot_general(uv, bk, (((0,), (0,)), ((), ())), preferred_element_type=F32)
        upd = s4 * p_all[:, ql]
        for h in range(QUAD):
            upd = upd + jnp.where(lane_head == h, full[RW_HEAD * h:RW_HEAD * (h + 1)], 0.0)
        s_scr[q] = upd

    @pl.when(c == pl.num_programs(2) - 1)
    def _():
        sf_ref[...] = s_scr[...]


def _rwkv_scan(r, v, kk, e, km, bvec, s0):
    bx, tx, _ = r.shape
    nc = tx // CHUNK
    nq = RW_HEADS // QUAD
    pos = lambda d, c: c + d * (nc - 1 - 2 * c)
    one = pl.BlockSpec((None, CHUNK, RW_WIDTH), lambda d, b, c: (b, pos(d, c), 0))
    two = pl.BlockSpec((None, None, CHUNK, RW_WIDTH), lambda d, b, c: (d, b, pos(d, c), 0))
    st = pl.BlockSpec((None, None, nq, RW_HEAD, QW), lambda d, b, c: (d, b, 0, 0, 0))
    return pl.pallas_call(
        _rw_scan_body,
        grid=(2, bx, nc),
        in_specs=[one, one, one, two, two, two, st],
        out_specs=[two, st],
        out_shape=[jax.ShapeDtypeStruct((2, bx, tx, RW_WIDTH), F32),
                   jax.ShapeDtypeStruct((2, bx, nq, RW_HEAD, QW), F32)],
        scratch_shapes=[pltpu.VMEM((nq, RW_HEAD, QW), F32)],
        compiler_params=_cparams(("arbitrary", "arbitrary", "arbitrary")),
        name="rwkv_scan",
    )(r, v, kk, e, km, bvec, s0)


def _merge_body(hy_ref, y_ref, bonus_ref, g_ref, gate_ref, x_ref, hyp_ref, rwp_ref, wo_ref, ones_ref,
                lng_ref, lnb_ref, m2_ref, n1_ref, n2_ref, sc4_ref, m3_ref, rw_ref,
                x1_ref, h2_ref, aff_ref):
    ones = ones_ref[...]
    ysum = y_ref[0] + y_ref[1]
    mean = _dot_hi(ysum, ones) * (1.0 / RW_HEAD)
    dev = ysum - mean
    var = _dot_hi(dev * dev, ones) * (1.0 / RW_HEAD)
    yn = dev * lax.rsqrt(var + RW_LN_EPS) * lng_ref[...] + lnb_ref[...]
    y_rw = (yn + bonus_ref[...]) * g_ref[...]
    gates = _sigmoid(gate_ref[...])
    mix = gates[:, :D_MODEL] * _dot_bf(hy_ref[...], hyp_ref[...]) + gates[:, D_MODEL:] * _dot_bf(y_rw, rwp_ref[...])
    mixed = _dot_bf(mix, wo_ref[...])
    x1 = x_ref[...] + m2_ref[...] * (_rms(mixed) * n1_ref[...])
    x1_ref[...] = x1
    h2 = _rms(x1) * n2_ref[...] * sc4_ref[...] + m3_ref[...]
    h2_ref[...] = h2
    logits = _dot_hi(h2, rw_ref[...])
    lane = lax.broadcasted_iota(I32, logits.shape, 1)
    logits = jnp.where(lane < N_EXPERTS, logits, -1e30)
    ex = jnp.exp(logits - jnp.max(logits, axis=1, keepdims=True))
    aff_ref[...] = ex / jnp.sum(ex, axis=1, keepdims=True)


def _merge(out_hy, y, bonus, g, gate_cols, x, hy_proj, rw_proj, w_out, ln_g, ln_b, m2, n1post, n2pre, sc4, m3,
           router_w):
    tt = 256
    tok = lambda w: pl.BlockSpec((None, tt, w), lambda b, t: (b, t, 0))
    full = lambda a: pl.BlockSpec(a.shape, lambda b, t: (0,) * a.ndim)
    per_b = pl.BlockSpec((None, 1, D_MODEL), lambda b, t: (b, 0, 0))
    vec = lambda a, n: a.reshape(1, n)
    consts = (hy_proj.astype(BF16), rw_proj.astype(BF16), w_out.astype(BF16), _head_ones(),
              vec(ln_g, RW_WIDTH), vec(ln_b, RW_WIDTH))
    router128 = _pad2(router_w, D_MODEL, 128)
    return pl.pallas_call(
        _merge_body,
        grid=(BATCH, SEQ // tt),
        in_specs=[tok(HY_WIDTH),
                  pl.BlockSpec((2, None, tt, RW_WIDTH), lambda b, t: (0, b, t, 0)),
                  tok(RW_WIDTH), tok(RW_WIDTH), tok(GATE_COLS), tok(D_MODEL)]
                 + [full(a) for a in consts]
                 + [per_b, full(vec(n1post, D_MODEL)), full(vec(n2pre, D_MODEL)), per_b, per_b,
                    full(router128)],
        out_specs=[tok(D_MODEL), tok(D_MODEL), tok(128)],
        out_shape=[jax.ShapeDtypeStruct((BATCH, SEQ, D_MODEL), F32),
                   jax.ShapeDtypeStruct((BATCH, SEQ, D_MODEL), F32),
                   jax.ShapeDtypeStruct((BATCH, SEQ, 128), F32)],
        compiler_params=_cparams(("arbitrary", "arbitrary"), VMEM_LIMIT),
        name="merge_router",
    )(out_hy, y, bonus, g, gate_cols, x, *consts, m2, vec(n1post, D_MODEL), vec(n2pre, D_MODEL), sc4, m3,
      router128)


def _topk_body(aff_ref, idx_ref, gate_ref, key_scr):
    def bisect(_, carry):
        lo, hi = carry
        mid = lo + ((hi - lo) >> 1)
        n_ge = jnp.sum(jnp.where(aff_ref[...] >= pltpu.bitcast(mid, F32), 1.0, 0.0), axis=0, keepdims=True)
        ok = n_ge >= float(CAP)
        return jnp.where(ok, mid, lo), jnp.where(ok, hi, mid)

    lo0 = jnp.zeros((1, 128), I32)
    hi0 = jnp.full((1, 128), 0x7F800000, I32)
    thr_bits, _ = lax.fori_loop(0, 32, bisect, (lo0, hi0))
    thr = pltpu.bitcast(thr_bits, F32)
    need = float(CAP) - jnp.sum(jnp.where(aff_ref[...] > thr, 1.0, 0.0), axis=0, keepdims=True)

    ri = lax.broadcasted_iota(I32, (128, 128), 0)
    ci = lax.broadcasted_iota(I32, (128, 128), 1)
    lower = jnp.where(ri >= ci, 1.0, 0.0).astype(BF16)
    off_eq = jnp.zeros((1, 128), F32)
    off_sel = jnp.zeros((1, 128), F32)
    for blk in range(SEQ // 128):
        sl = slice(128 * blk, 128 * (blk + 1))
        aff_b = aff_ref[sl, :]
        eq_b = jnp.where(aff_b == thr, 1.0, 0.0)
        inc = jnp.dot(lower, eq_b.astype(BF16), preferred_element_type=F32) + off_eq
        off_eq = inc[127:128, :]
        sel_b = jnp.where(aff_b > thr, 1.0, 0.0) + eq_b * jnp.where(inc - eq_b < need, 1.0, 0.0)
        cnt = jnp.dot(lower, sel_b.astype(BF16), preferred_element_type=F32) + off_sel
        off_sel = cnt[127:128, :]
        key_scr[sl, :] = cnt * sel_b

    slot1 = lax.broadcasted_iota(I32, (128, CAP), 1).astype(F32) + 1.0
    lane = lax.broadcasted_iota(I32, (128, 128), 1)
    pos = lax.broadcasted_iota(I32, (128, 128), 0).astype(F32)
    for ex in range(N_EXPERTS):
        col = slice(ex, ex + 1)

        def per_block(tb, acc, col=col):
            rows = pl.ds(pl.multiple_of(tb * 128, 128), 128)
            onehot = jnp.where(key_scr[rows, col] == slot1, 1.0, 0.0).astype(BF16)
            a0 = aff_ref[rows, col]
            a_hi = a0.astype(BF16).astype(F32)
            a1 = a0 - a_hi
            a_mid = a1.astype(BF16).astype(F32)
            a_lo = a1 - a_mid
            blk = (tb + jnp.zeros((128, 128), I32)).astype(F32)
            payload = jnp.where(lane == 0, pos, jnp.where(lane == 1, blk, jnp.where(
                lane == 2, a_hi, jnp.where(lane == 3, a_mid, jnp.where(lane == 4, a_lo, 0.0)))))
            moved = lax.dot_general(payload.astype(BF16), onehot, (((0,), (0,)), ((), ())),
                                    preferred_element_type=F32)
            return acc + moved[0:8, :]

        acc = lax.fori_loop(0, SEQ // 128, per_block, jnp.zeros((8, CAP), F32))
        idx_ref[col, :] = (128.0 * acc[1:2, :] + acc[0:1, :]).astype(I32)
        gate_ref[col, :] = (acc[2:3, :] + acc[3:4, :]) + acc[4:5, :]


def _expert_choice(aff):
    return pl.pallas_call(
        _topk_body,
        grid=(BATCH,),
        in_specs=[pl.BlockSpec((None, SEQ, 128), lambda b: (b, 0, 0))],
        out_specs=[pl.BlockSpec((None, N_EXPERTS, CAP), lambda b: (b, 0, 0))] * 2,
        out_shape=[jax.ShapeDtypeStruct((BATCH, N_EXPERTS, CAP), I32),
                   jax.ShapeDtypeStruct((BATCH, N_EXPERTS, CAP), F32)],
        scratch_shapes=[pltpu.VMEM((SEQ, 128), F32)],
        compiler_params=_cparams(("arbitrary",), VMEM_LIMIT),
        name="expert_choice",
    )(aff)


HALF = D_MODEL // 2


def _gather_body(idx_ref, h_ref, o_ref, buf):
    sub = lax.broadcasted_iota(I32, (8, HALF), 0)

    def group(gi, carry):
        base = pl.multiple_of(gi * 8, 8)
        acc = jnp.zeros((8, HALF), F32)
        for j in range(8):
            row = idx_ref[0, base + j]
            tile = h_ref[pl.ds(pl.multiple_of((row >> 3) << 3, 8), 8), :]
            acc = jnp.where(sub == j, pltpu.roll(tile, (j - row) & 7, 0), acc)
        buf[pl.ds(base, 8), :] = acc
        return carry

    lax.fori_loop(0, CAP // 8, group, 0)
    o_ref[...] = buf[...].astype(BF16)


def _gather(idx3, h2):
    return pl.pallas_call(
        _gather_body,
        grid=(BATCH, 2, N_EXPERTS),
        in_specs=[pl.BlockSpec((None, 1, CAP), lambda b, dh, e: (b * N_EXPERTS + e, 0, 0),
                               memory_space=pltpu.SMEM),
                  pl.BlockSpec((None, SEQ, HALF), lambda b, dh, e: (b, 0, dh))],
        out_specs=pl.BlockSpec((None, None, CAP, HALF), lambda b, dh, e: (b, e, 0, dh)),
        out_shape=jax.ShapeDtypeStruct((BATCH, N_EXPERTS, CAP, D_MODEL), BF16),
        scratch_shapes=[pltpu.VMEM((CAP, HALF), F32)],
        compiler_params=_cparams(("arbitrary", "arbitrary", "arbitrary"), VMEM_LIMIT),
        name="moe_gather",
    )(idx3, h2)


def _ffn_body(x_ref, w1_ref, w3_ref, w2_ref, gt_ref, o_ref):
    f = pl.program_id(2)
    x = x_ref[...]
    a = jnp.dot(x, w1_ref[...].astype(BF16), preferred_element_type=F32)
    b = jnp.dot(x, w3_ref[...].astype(BF16), preferred_element_type=F32)
    hid = (a * _sigmoid(a) * b).astype(BF16)
    part = jnp.dot(hid, w2_ref[...].astype(BF16), preferred_element_type=F32)

    @pl.when(f == 0)
    def _():
        o_ref[...] = part

    @pl.when(f > 0)
    def _():
        o_ref[...] += part

    @pl.when(f == pl.num_programs(2) - 1)
    def _():
        o_ref[...] = o_ref[...] * gt_ref[...]


def _experts(xe, w1, w3, w2, gates4):
    tf = 512
    return pl.pallas_call(
        _ffn_body,
        grid=(BATCH, N_EXPERTS, D_EXPERT // tf),
        in_specs=[pl.BlockSpec((None, None, CAP, D_MODEL), lambda b, e, f: (b, e, 0, 0)),
                  pl.BlockSpec((None, D_MODEL, tf), lambda b, e, f: (e, 0, f)),
                  pl.BlockSpec((None, D_MODEL, tf), lambda b, e, f: (e, 0, f)),
                  pl.BlockSpec((None, tf, D_MODEL), lambda b, e, f: (e, f, 0)),
                  pl.BlockSpec((None, None, CAP, 1), lambda b, e, f: (b, e, 0, 0))],
        out_specs=pl.BlockSpec((None, None, CAP, D_MODEL), lambda b, e, f: (b, e, 0, 0)),
        out_shape=jax.ShapeDtypeStruct((BATCH, N_EXPERTS, CAP, D_MODEL), F32),
        compiler_params=_cparams(("arbitrary", "arbitrary", "arbitrary"), VMEM_LIMIT),
        name="moe_experts",
    )(xe, w1, w3, w2, gates4)


def _scatter_body(idx_ref, ye_ref, o_ref):
    @pl.when(pl.program_id(2) == 0)
    def _():
        o_ref[...] = jnp.zeros_like(o_ref)

    sub = lax.broadcasted_iota(I32, (8, HALF), 0)

    def group(gi, carry):
        base = pl.multiple_of(gi * 8, 8)
        rows8 = ye_ref[pl.ds(base, 8), :]
        for j in range(8):
            row = idx_ref[0, base + j]
            tile = pl.ds(pl.multiple_of((row >> 3) << 3, 8), 8)
            add = jnp.where(sub == (row & 7), jnp.broadcast_to(rows8[j:j + 1, :], (8, HALF)), 0.0)
            o_ref[tile, :] = o_ref[tile, :] + add
        return carry

    lax.fori_loop(0, CAP // 8, group, 0)


def _scatter(idx3, ye):
    return pl.pallas_call(
        _scatter_body,
        grid=(BATCH, 2, N_EXPERTS),
        in_specs=[pl.BlockSpec((None, 1, CAP), lambda b, dh, e: (b * N_EXPERTS + e, 0, 0),
                               memory_space=pltpu.SMEM),
                  pl.BlockSpec((None, None, CAP, HALF), lambda b, dh, e: (b, e, 0, dh))],
        out_specs=pl.BlockSpec((None, SEQ, HALF), lambda b, dh, e: (b, 0, dh)),
        out_shape=jax.ShapeDtypeStruct((BATCH, SEQ, D_MODEL), F32),
        compiler_params=_cparams(("arbitrary", "arbitrary", "arbitrary"), VMEM_LIMIT),
        name="moe_scatter",
    )(idx3, ye)


def _final_body(x1_ref, moe_ref, m5_ref, n_ref, o_ref):
    o_ref[...] = x1_ref[...] + m5_ref[...] * (_rms(moe_ref[...]) * n_ref[...])


def _final(x1, moe, m5, n2post):
    tt = 512
    tok = pl.BlockSpec((None, tt, D_MODEL), lambda b, t: (b, t, 0))
    return pl.pallas_call(
        _final_body,
        grid=(BATCH, SEQ // tt),
        in_specs=[tok, tok, pl.BlockSpec((None, 1, D_MODEL), lambda b, t: (b, 0, 0)),
                  pl.BlockSpec((1, D_MODEL), lambda b, t: (0, 0))],
        out_specs=tok,
        out_shape=jax.ShapeDtypeStruct((BATCH, SEQ, D_MODEL), F32),
        compiler_params=_cparams(("arbitrary", "arbitrary")),
        name="moe_residual",
    )(x1, moe, m5, n2post.reshape(1, D_MODEL))


def _layer(x, c, ctx, c_ctx, p):
    c8 = jnp.zeros((8, D_MODEL), F32).at[:BATCH].set(c).at[BATCH].set(c_ctx)
    mod = _modulation(c8, p['mod_w'], p['mod_b'])
    m_lat = [mod[:BATCH, i * D_MODEL:(i + 1) * D_MODEL].reshape(BATCH, 1, D_MODEL) for i in range(6)]
    m_ctx = [jnp.broadcast_to(mod[BATCH, i * D_MODEL:(i + 1) * D_MODEL], (BATCH, 1, D_MODEL)) for i in range(2)]

    w_in = p['w_in'].astype(BF16)
    w_hy = w_in[:, :HY_COLS]
    w_rw = w_in[:, HY_COLS:HY_COLS + RW_COLS]
    w_gate = w_in[:, HY_COLS + RW_COLS:]
    rw_args = (p['rw_mu'], p['rw_w_up'], p['rw_a_up'], p['rw_w0'], p['rw_a0'], p['rw_g_up'],
               p['rw_k_k'], p['rw_k_a'], p['rw_r_k'])

    (ctx_cols,) = _in_projection(ctx, p['norm1_pre'], 1.0 + m_ctx[1], m_ctx[0], [w_rw], CTX_LEN, 1)
    cr, cv, ckk, ce, ckm, cb, _, _ = _rwkv_prep(ctx_cols, *rw_args)
    zero_state = jnp.zeros((2, BATCH, RW_HEADS // QUAD, RW_HEAD, QW), F32)
    _, ctx_states = _rwkv_scan(cr, cv, ckk, ce, ckm, cb, zero_state)

    hy_cols, rw_cols, gate_cols = _in_projection(x, p['norm1_pre'], 1.0 + m_lat[1], m_lat[0],
                                                 [w_hy, w_rw, w_gate], 512, 2)
    out_hy = _hyena_branch(hy_cols, p['hy_conv_w'], p['hy_conv_b'], p['hy_ffn_w1'], p['hy_ffn_b1'],
                           p['hy_ffn_w2'], p['hy_ffn_b2'], p['hy_ffn_w3'], p['hy_freq'], p['hy_bias'])
    r, v, kk, e, km, bvec, bonus, g = _rwkv_prep(rw_cols, *rw_args)
    y, _ = _rwkv_scan(r, v, kk, e, km, bvec, ctx_states)

    x1, h2, aff = _merge(out_hy, y, bonus, g, gate_cols, x, p['hy_proj'], p['rw_proj'], p['w_out'],
                           p['rw_ln_g'], p['rw_ln_b'], m_lat[2], p['norm1_post'], p['norm2_pre'],
                           1.0 + m_lat[4], m_lat[3], p['router_w'])
    idx, gates = _expert_choice(aff)
    idx3 = idx.reshape(BATCH * N_EXPERTS, 1, CAP)
    gates4 = gates.reshape(BATCH, N_EXPERTS, CAP, 1)
    xe = _gather(idx3, h2)
    ye = _experts(xe, p['exp_w1'], p['exp_w3'], p['exp_w2'], gates4)
    moe = _scatter(idx3, ye)
    return _final(x1, moe, m_lat[5], p['norm2_post'])


def kernel(x, c, ctx, c_ctx, mod_w, mod_b, norm1_pre, norm1_post, norm2_pre, norm2_post, w_in, hy_conv_w, hy_conv_b, hy_ffn_w1, hy_ffn_b1, hy_ffn_w2, hy_ffn_b2, hy_ffn_w3, hy_freq, hy_bias, hy_proj, rw_mu, rw_w0, rw_w_up, rw_a0, rw_a_up, rw_g_up, rw_k_k, rw_k_a, rw_r_k, rw_ln_g, rw_ln_b, rw_proj, w_out, router_w, exp_w1, exp_w3, exp_w2):
    names = ('mod_w', 'mod_b', 'norm1_pre', 'norm1_post', 'norm2_pre', 'norm2_post', 'w_in', 'hy_conv_w',
             'hy_conv_b', 'hy_ffn_w1', 'hy_ffn_b1', 'hy_ffn_w2', 'hy_ffn_b2', 'hy_ffn_w3', 'hy_freq', 'hy_bias',
             'hy_proj', 'rw_mu', 'rw_w0', 'rw_w_up', 'rw_a0', 'rw_a_up', 'rw_g_up', 'rw_k_k', 'rw_k_a', 'rw_r_k',
             'rw_ln_g', 'rw_ln_b', 'rw_proj', 'w_out', 'router_w', 'exp_w1', 'exp_w3', 'exp_w2')
    vals = (mod_w, mod_b, norm1_pre, norm1_post, norm2_pre, norm2_post, w_in, hy_conv_w, hy_conv_b, hy_ffn_w1,
            hy_ffn_b1, hy_ffn_w2, hy_ffn_b2, hy_ffn_w3, hy_freq, hy_bias, hy_proj, rw_mu, rw_w0, rw_w_up, rw_a0,
            rw_a_up, rw_g_up, rw_k_k, rw_k_a, rw_r_k, rw_ln_g, rw_ln_b, rw_proj, w_out, router_w, exp_w1, exp_w3,
            exp_w2)
    depth = mod_w.shape[0]
    assert depth == 1, "single-layer block: the context stream only provides scan start states"
    p = {n: a[0] for n, a in zip(names, vals)}
    return _layer(x, c, ctx, c_ctx, p)
```

```python
import functools
import math

import numpy as np
import jax
import jax.numpy as jnp
from jax import lax
from jax.experimental import pallas as pl
from jax.experimental.pallas import tpu as pltpu

F32 = jnp.float32
BF16 = jnp.bfloat16
I32 = jnp.int32
HI = lax.Precision.HIGHEST

D_MODEL = 1024
BATCH = 4
SEQ = 8192
CTX_LEN = 256

HY_WIDTH = 512
HY_BANDS = 16
HY_FILTER_ORDER = 64
HY_DECAY_TARGET = 1e-2
HY_DECAY_SHORT = 0.3
HY_DECAY_LONG = 1.5

RW_HEADS = 8
RW_HEAD = 64
RW_WIDTH = RW_HEADS * RW_HEAD
LORA_W = 64
LORA_A = 64
LORA_G = 128
RW_LN_EPS = 64e-5

HY_COLS = 3 * HY_WIDTH
RW_COLS = 3 * RW_WIDTH + LORA_W + LORA_A + LORA_G
GATE_COLS = 2 * D_MODEL

N_EXPERTS = 16
D_EXPERT = 2048
EC_CAPACITY = 2
NORM_EPS = 1e-6
CAP = EC_CAPACITY * SEQ // N_EXPERTS

NFFT = 2 * SEQ
FFT_R = 128
FFT_H = FFT_R // 2
FFT_J = 8
CHUNK = 64
QUAD = 4
QW = QUAD * RW_HEAD
NQ = RW_HEADS // QUAD

VMEM_LIMIT = 56 * 1024 * 1024


def _cparams(sem, vmem=None):
    return pltpu.CompilerParams(dimension_semantics=sem, vmem_limit_bytes=vmem)


def _dot_hi(a, b):
    return jnp.dot(a, b, precision=HI, preferred_element_type=F32)


def _bdot(a, b):
    return jnp.dot(a, b, preferred_element_type=F32)


def _split(a):
    hi = a.astype(BF16)
    return hi, (a - hi.astype(F32)).astype(BF16)


def _dot_c2(m_hi, m_lo, x):
    xb = x.astype(BF16)
    return _bdot(m_hi, xb) + _bdot(m_lo, xb)


def _dot_w3(x, w_hi, w_lo):
    xh, xl = _split(x)
    return _bdot(xh, w_hi) + (_bdot(xl, w_hi) + _bdot(xh, w_lo))


def _head_sums(x, ones_bf):
    xh, xl = _split(x)
    return _bdot(xh, ones_bf) + _bdot(xl, ones_bf)


def _rms(x):
    return x * lax.rsqrt(jnp.mean(x * x, axis=-1, keepdims=True) + NORM_EPS)


def _sigmoid(x):
    return 1.0 / (1.0 + jnp.exp(-x))


def _pad2(a, rows, cols):
    return jnp.zeros((rows, cols), F32).at[:a.shape[0], :a.shape[1]].set(a)


def _mod_body(c_ref, w_ref, b_ref, o_ref):
    c = c_ref[...]
    o_ref[...] = _dot_hi(c * _sigmoid(c), w_ref[...]) + b_ref[...]


def _modulation(c8, mod_w, mod_b):
    n = mod_w.shape[1]
    tn = 1536
    return pl.pallas_call(
        _mod_body,
        grid=(n // tn,),
        in_specs=[pl.BlockSpec((8, D_MODEL), lambda j: (0, 0)),
                  pl.BlockSpec((D_MODEL, tn), lambda j: (0, j)),
                  pl.BlockSpec((1, tn), lambda j: (0, j))],
        out_specs=pl.BlockSpec((8, tn), lambda j: (0, j)),
        out_shape=jax.ShapeDtypeStruct((8, n), F32),
        compiler_params=_cparams(("arbitrary",)),
        name="modulation",
    )(c8, mod_w, mod_b.reshape(1, n))


def _inproj_body(nw, x_ref, g_ref, sc_ref, sh_ref, *refs):
    w_refs, o_refs, h_ref = refs[:nw], refs[nw:2 * nw], refs[2 * nw]

    @pl.when(pl.program_id(2) == 0)
    def _():
        h = _rms(x_ref[...]) * g_ref[...] * sc_ref[...] + sh_ref[...]
        h_ref[...] = h.astype(BF16)

    h = h_ref[...]
    for w_ref, o_ref in zip(w_refs, o_refs):
        o_ref[...] = jnp.dot(h, w_ref[...], preferred_element_type=F32)


def _in_projection(x, g, scale1p, shift, weights, tm, nsplit):
    bx, tx, _ = x.shape
    nw = len(weights)
    tns = [w.shape[1] // nsplit for w in weights]
    in_specs = [pl.BlockSpec((None, tm, D_MODEL), lambda b, m, n: (b, m, 0)),
                pl.BlockSpec((1, D_MODEL), lambda b, m, n: (0, 0)),
                pl.BlockSpec((None, 1, D_MODEL), lambda b, m, n: (b, 0, 0)),
                pl.BlockSpec((None, 1, D_MODEL), lambda b, m, n: (b, 0, 0))]
    in_specs += [pl.BlockSpec((D_MODEL, tn), lambda b, m, n: (0, n)) for tn in tns]
    out_specs = [pl.BlockSpec((None, tm, tn), lambda b, m, n: (b, m, n)) for tn in tns]
    out_shape = [jax.ShapeDtypeStruct((bx, tx, w.shape[1]), F32) for w in weights]
    return pl.pallas_call(
        functools.partial(_inproj_body, nw),
        grid=(bx, tx // tm, nsplit),
        in_specs=in_specs,
        out_specs=out_specs,
        out_shape=out_shape,
        scratch_shapes=[pltpu.VMEM((tm, D_MODEL), BF16)],
        compiler_params=_cparams(("arbitrary", "arbitrary", "arbitrary"), VMEM_LIMIT),
        name="in_projection",
    )(x, g.reshape(1, D_MODEL), scale1p, shift, *weights)


def _neighbours(cur, pv_ref, nx_ref):
    tt = cur.shape[0]
    t = pl.program_id(1)
    first = (t > 0).astype(F32)
    last = (t < pl.num_programs(1) - 1).astype(F32)
    prev_row = pv_ref[7:8, :] * first
    next_row = nx_ref[0:1, :] * last
    rows = lax.broadcasted_iota(I32, cur.shape, 0)
    prev = jnp.where(rows == 0, prev_row, pltpu.roll(cur, 1, 0))
    nxt = jnp.where(rows == tt - 1, next_row, pltpu.roll(cur, tt - 1, 0))
    return prev, nxt


def _halo_specs(tt, t_total, width):
    r8 = tt // 8
    last8 = t_total // 8 - 1
    return [pl.BlockSpec((None, tt, width), lambda b, t: (b, t, 0)),
            pl.BlockSpec((None, 8, width), lambda b, t: (b, jnp.maximum(t * r8 - 1, 0), 0)),
            pl.BlockSpec((None, 8, width), lambda b, t: (b, jnp.minimum((t + 1) * r8, last8), 0))]


def _hy_pre_body(cur_ref, pv_ref, nx_ref, w_ref, b_ref, z_ref, x0_ref):
    cur = cur_ref[...]
    prev, nxt = _neighbours(cur, pv_ref, nx_ref)
    u = w_ref[0:1, :] * prev + w_ref[1:2, :] * cur + w_ref[2:3, :] * nxt + b_ref[...]
    z_ref[...] = u[:, 2 * HY_WIDTH:] * u[:, HY_WIDTH:2 * HY_WIDTH]
    x0_ref[...] = u[:, :HY_WIDTH]


def _hyena_pre(hy_cols, conv_w, conv_b):
    tt = 512
    spec_o = pl.BlockSpec((None, tt, HY_WIDTH), lambda b, t: (b, t, 0))
    return pl.pallas_call(
        _hy_pre_body,
        grid=(BATCH, SEQ // tt),
        in_specs=_halo_specs(tt, SEQ, HY_COLS) + [
            pl.BlockSpec((3, HY_COLS), lambda b, t: (0, 0)),
            pl.BlockSpec((1, HY_COLS), lambda b, t: (0, 0))],
        out_specs=[spec_o, spec_o],
        out_shape=[jax.ShapeDtypeStruct((BATCH, SEQ, HY_WIDTH), F32)] * 2,
        compiler_params=_cparams(("arbitrary", "arbitrary"), VMEM_LIMIT),
        name="hyena_pre",
    )(hy_cols, hy_cols, hy_cols, conv_w, conv_b.reshape(1, HY_COLS))


def _filt_body(bands_ref, w1t_ref, w1c_ref, w1s_ref, b1_ref, w2_ref, b2_ref, w3_ref, fr_ref, dl_ref,
               f_ref, s_ref):
    i = pl.program_id(0)
    tt = f_ref.shape[0]
    rowi = lax.broadcasted_iota(I32, (tt, 1), 0) + i * tt
    rowf = rowi.astype(F32)
    tpos = rowf * (1.0 / (SEQ - 1))
    arg = (rowf * (2.0 * math.pi / SEQ)) * bands_ref[...]
    pre = tpos * w1t_ref[...] + _dot_hi(jnp.cos(arg), w1c_ref[...]) - _dot_hi(jnp.sin(arg), w1s_ref[...]) + b1_ref[...]
    freq = fr_ref[...]
    hid = jnp.sin(freq * pre)
    hid = jnp.sin(freq * (_dot_hi(hid, w2_ref[...]) + b2_ref[...]))
    filt = _dot_hi(hid, w3_ref[...]) * jnp.exp(-tpos * dl_ref[...])
    lane = lax.broadcasted_iota(I32, filt.shape, 1)
    rows = lax.broadcasted_iota(I32, filt.shape, 0) + i * tt
    filt = jnp.where((rows == 0) & (lane >= HY_WIDTH), 0.0, filt)
    f_ref[...] = filt

    @pl.when(i == 0)
    def _():
        s_ref[...] = jnp.zeros_like(s_ref)

    s_ref[...] += jnp.sum(jnp.abs(filt), axis=0, keepdims=True)


def _hyena_filter(w1, b1, w2, b2, w3, freq):
    tt = 512
    bands = np.zeros((1, 128), np.float32)
    bands[0, :HY_BANDS] = np.linspace(1e-4, HY_BANDS - 1, HY_BANDS, dtype=np.float32)
    deltas = np.abs(np.linspace(math.log(HY_DECAY_TARGET) / HY_DECAY_LONG,
                                math.log(HY_DECAY_TARGET) / HY_DECAY_SHORT, HY_WIDTH, dtype=np.float32))
    dl = np.concatenate([deltas, deltas])[None, :].astype(np.float32)
    args = (jnp.asarray(bands),
            _pad2(w1[0:1], 1, 128),
            _pad2(w1[1:1 + HY_BANDS], 128, 128),
            _pad2(w1[1 + HY_BANDS:], 128, 128),
            _pad2(b1[None], 1, 128),
            _pad2(w2, 128, 128),
            _pad2(b2[None], 1, 128),
            _pad2(w3, 128, 2 * HY_WIDTH),
            _pad2(freq[None], 1, 128),
            jnp.asarray(dl))
    full = lambda a: pl.BlockSpec(a.shape, lambda i: (0, 0))
    return pl.pallas_call(
        _filt_body,
        grid=(SEQ // tt,),
        in_specs=[full(a) for a in args],
        out_specs=[pl.BlockSpec((tt, 2 * HY_WIDTH), lambda i: (i, 0)),
                   pl.BlockSpec((1, 2 * HY_WIDTH), lambda i: (0, 0))],
        out_shape=[jax.ShapeDtypeStruct((SEQ, 2 * HY_WIDTH), F32),
                   jax.ShapeDtypeStruct((1, 2 * HY_WIDTH), F32)],
        compiler_params=_cparams(("arbitrary",)),
        name="hyena_filter",
    )(*args)


def _fft_tables():
    k = np.arange(FFT_R)
    n = np.arange(FFT_H)
    th = 2.0 * np.pi * np.outer(k, n) / FFT_R
    f1 = np.stack([np.cos(th), -np.sin(th)])
    eye = np.eye(FFT_J)
    g1 = np.einsum('pkn,jm->pkjnm', f1, eye).reshape(2 * FFT_R * FFT_J, FFT_H * FFT_J)
    f3 = np.stack([np.cos(th.T), -np.sin(th.T)]) / NFFT
    g3 = np.einsum('pnk,jm->njpkm', f3, eye).reshape(FFT_H * FFT_J, 2 * FFT_R * FFT_J)
    th2 = 2.0 * np.pi * np.outer(k, k) / FFT_R
    c, s = np.cos(th2), np.sin(th2)
    fc = np.block([[c, s], [-s, c]])
    fi = np.block([[c, -s], [s, c]])
    return tuple(_split(jnp.asarray(a, F32)) for a in (g1, g3, fc, fi))


def _twiddle(rows_k, n2):
    ph = ((rows_k * n2) & (NFFT - 1)).astype(F32) * (2.0 * math.pi / NFFT)
    return jnp.cos(ph), jnp.sin(ph)


def _fft1_body(z_ref, gh_ref, gl_ref, o_ref):
    nb = pl.program_id(1)
    c = z_ref.shape[-1]
    z = z_ref[...].reshape(FFT_H * FFT_J, c)
    a = _dot_c2(gh_ref[...], gl_ref[...], z)
    half = FFT_R * FFT_J
    ar, ai = a[:half], a[half:]
    r = lax.broadcasted_iota(I32, (half, 1), 0)
    cs, sn = _twiddle(r >> 3, nb * FFT_J + (r & (FFT_J - 1)))
    o_ref[0] = (ar * cs + ai * sn).reshape(FFT_R, FFT_J, c)
    o_ref[1] = (ai * cs - ar * sn).reshape(FFT_R, FFT_J, c)


def _fft_stage1(z4, g1):
    bz, _, _, c = z4.shape
    tab = pl.BlockSpec(g1[0].shape, lambda b, n: (0, 0))
    return pl.pallas_call(
        _fft1_body,
        grid=(bz, FFT_R // FFT_J),
        in_specs=[pl.BlockSpec((None, FFT_H, FFT_J, c), lambda b, n: (b, 0, n, 0)), tab, tab],
        out_specs=pl.BlockSpec((None, 2, FFT_R, FFT_J, c), lambda b, n: (b, 0, 0, n, 0)),
        out_shape=jax.ShapeDtypeStruct((bz, 2, FFT_R, FFT_R, c), F32),
        compiler_params=_cparams(("arbitrary", "arbitrary"), VMEM_LIMIT),
        name="fft_stage1",
    )(z4, *g1)


def _filt_spec_body(a_ref, fh_ref, fl_ref, h_ref):
    x = _dot_c2(fh_ref[...], fl_ref[...], jnp.concatenate([a_ref[0], a_ref[1]], axis=0))
    xr, xi = x[:FFT_R], x[FFT_R:]
    h_ref[0] = xr[:, :HY_WIDTH] + xr[:, HY_WIDTH:]
    h_ref[1] = xi[:, :HY_WIDTH] - xi[:, HY_WIDTH:]


def _filter_spectrum(a, fc):
    tab = pl.BlockSpec(fc[0].shape, lambda k: (0, 0))
    return pl.pallas_call(
        _filt_spec_body,
        grid=(FFT_R,),
        in_specs=[pl.BlockSpec((None, 2, None, FFT_R, 2 * HY_WIDTH), lambda k: (0, 0, k, 0, 0)), tab, tab],
        out_specs=pl.BlockSpec((None, 2, FFT_R, HY_WIDTH), lambda k: (k, 0, 0, 0)),
        out_shape=jax.ShapeDtypeStruct((FFT_R, 2, FFT_R, HY_WIDTH), F32),
        compiler_params=_cparams(("arbitrary",)),
        name="filter_spectrum",
    )(a, *fc)


def _fft_mid_body(a_ref, h_ref, fch_ref, fcl_ref, fih_ref, fil_ref, o_ref):
    k1 = pl.program_id(0)
    x = _dot_c2(fch_ref[...], fcl_ref[...], jnp.concatenate([a_ref[0], a_ref[1]], axis=0))
    xr, xi = x[:FFT_R], x[FFT_R:]
    hr, him = h_ref[0], h_ref[1]
    yr = xr * hr - xi * him
    yi = xr * him + xi * hr
    p = _dot_c2(fih_ref[...], fil_ref[...], jnp.concatenate([yr, yi], axis=0))
    pr, pim = p[:FFT_R], p[FFT_R:]
    cs, sn = _twiddle(lax.broadcasted_iota(I32, (FFT_R, 1), 0), k1)
    o_ref[0] = pr * cs - pim * sn
    o_ref[1] = pr * sn + pim * cs


def _fft_mid(a, h, fc, fi):
    bz = a.shape[0]
    blk = pl.BlockSpec((None, 2, None, FFT_R, HY_WIDTH), lambda k, b: (b, 0, k, 0, 0))
    tab = pl.BlockSpec(fc[0].shape, lambda k, b: (0, 0))
    return pl.pallas_call(
        _fft_mid_body,
        grid=(FFT_R, bz),
        in_specs=[blk, pl.BlockSpec((None, 2, FFT_R, HY_WIDTH), lambda k, b: (k, 0, 0, 0)), tab, tab, tab, tab],
        out_specs=blk,
        out_shape=jax.ShapeDtypeStruct(a.shape, F32),
        compiler_params=_cparams(("arbitrary", "arbitrary")),
        name="fft_mid",
    )(a, h, *fc, *fi)


def _ifft_body(q_ref, gh_ref, gl_ref, z_ref, x0_ref, inv_ref, hb_ref, o_ref):
    c = HY_WIDTH
    q = q_ref[...].reshape(2 * FFT_R * FFT_J, c)
    conv = _dot_c2(gh_ref[...], gl_ref[...], q) * inv_ref[...]
    z = z_ref[...].reshape(FFT_H * FFT_J, c)
    x0 = x0_ref[...].reshape(FFT_H * FFT_J, c)
    o_ref[...] = ((conv + hb_ref[...] * z) * x0).reshape(FFT_H, FFT_J, c)


def _ifft_gate(q, g3, z4, x04, inv_norm, hy_bias):
    row = pl.BlockSpec((None, FFT_H, FFT_J, HY_WIDTH), lambda b, n: (b, 0, n, 0))
    vec = pl.BlockSpec((1, HY_WIDTH), lambda b, n: (0, 0))
    tab = pl.BlockSpec(g3[0].shape, lambda b, n: (0, 0))
    return pl.pallas_call(
        _ifft_body,
        grid=(BATCH, FFT_R // FFT_J),
        in_specs=[pl.BlockSpec((None, 2, FFT_R, FFT_J, HY_WIDTH), lambda b, n: (b, 0, 0, n, 0)),
                  tab, tab, row, row, vec, vec],
        out_specs=row,
        out_shape=jax.ShapeDtypeStruct(z4.shape, F32),
        compiler_params=_cparams(("arbitrary", "arbitrary"), VMEM_LIMIT),
        name="ifft_gate",
    )(q, *g3, z4, x04, inv_norm, hy_bias)


def _hyena_branch(hy_cols, conv_w, conv_b, w1, b1, w2, b2, w3, freq, hy_bias):
    g1, g3, fc, fi = _fft_tables()
    z, x0 = _hyena_pre(hy_cols, conv_w, conv_b)
    filt, l1 = _hyena_filter(w1, b1, w2, b2, w3, freq)
    inv_norm = 1.0 / (l1[:, :HY_WIDTH] + l1[:, HY_WIDTH:])
    fa = _fft_stage1(filt.reshape(1, FFT_H, FFT_R, 2 * HY_WIDTH), g1)
    h = _filter_spectrum(fa, fc)
    z4 = z.reshape(BATCH, FFT_H, FFT_R, HY_WIDTH)
    za = _fft_stage1(z4, g1)
    q = _fft_mid(za, h, fc, fi)
    out = _ifft_gate(q, g3, z4, x0.reshape(z4.shape), inv_norm, hy_bias.reshape(1, HY_WIDTH))
    return out.reshape(BATCH, SEQ, HY_WIDTH)


def _rw_prep_body(cur_ref, pv_ref, nx_ref, mu_ref, w2h_ref, w2l_ref, w0_ref, a0_ref, guph_ref, gupl_ref, kkw_ref,
                  ka_ref, rk_ref, ones_ref, r_o, v_o, kk_o, e_o, km_o, b_o, bonus_o, g_o):
    cur = cur_ref[...]
    prev, nxt = _neighbours(cur, pv_ref, nx_ref)
    z = cur + mu_ref[0:1, :] * (prev - cur) + mu_ref[1:2, :] * (nxt - cur)
    w_ = RW_WIDTH
    r, k, v = z[:, :w_], z[:, w_:2 * w_], z[:, 2 * w_:3 * w_]
    ll = z[:, 3 * w_:3 * w_ + 128]
    lg = z[:, 3 * w_ + 128:]
    lane = lax.broadcasted_iota(I32, ll.shape, 1)
    tl = jnp.where(lane < LORA_W, jnp.tanh(ll), ll)
    ones = ones_ref[...]
    kkr = k * kkw_ref[...]
    kk = kkr * lax.rsqrt(jnp.maximum(_head_sums(kkr * kkr, ones), 1e-24))
    bonus = jnp.zeros_like(r)
    for d in range(2):
        proj = _dot_w3(tl, w2h_ref[d], w2l_ref[d])
        wl = proj[:, :w_] + w0_ref[d:d + 1, :]
        al = proj[:, w_:] + a0_ref[d:d + 1, :]
        nwl = -wl
        w = -(jnp.maximum(nwl, 0.0) + jnp.log(1.0 + jnp.exp(-jnp.abs(nwl)))) - 0.5
        a = _sigmoid(al)
        km = k * (1.0 + (a - 1.0) * ka_ref[...])
        e_o[d] = jnp.exp(w)
        km_o[d] = km
        b_o[d] = kk * a
        bonus = bonus + _head_sums(r * km * rk_ref[...], ones)
    r_o[...] = r
    v_o[...] = v
    kk_o[...] = kk
    bonus_o[...] = bonus * v
    g_o[...] = _dot_w3(_sigmoid(lg), guph_ref[...], gupl_ref[...])


def _head_ones():
    h = np.arange(RW_WIDTH) // RW_HEAD
    return jnp.asarray((h[:, None] == h[None, :]).astype(np.float32), BF16)


def _rwkv_prep(rw_cols, mu, w_up, a_up, w0, a0, g_up, k_k, k_a, r_k):
    bx, tx, _ = rw_cols.shape
    tt = 256
    w2 = jnp.zeros((2, 128, 2 * RW_WIDTH), F32)
    w2 = w2.at[:, :LORA_W, :RW_WIDTH].set(w_up).at[:, LORA_W:, RW_WIDTH:].set(a_up)
    vec = lambda a: a.reshape(1, RW_WIDTH)
    full2 = lambda a: pl.BlockSpec(a.shape, lambda b, t: (0,) * a.ndim)
    args = (mu, *_split(w2), w0, a0, *_split(g_up), vec(k_k), vec(k_a), vec(r_k), _head_ones())
    one = pl.BlockSpec((None, tt, RW_WIDTH), lambda b, t: (b, t, 0))
    two = pl.BlockSpec((2, None, tt, RW_WIDTH), lambda b, t: (0, b, t, 0))
    s1 = jax.ShapeDtypeStruct((bx, tx, RW_WIDTH), F32)
    s2 = jax.ShapeDtypeStruct((2, bx, tx, RW_WIDTH), F32)
    return pl.pallas_call(
        _rw_prep_body,
        grid=(bx, tx // tt),
        in_specs=_halo_specs(tt, tx, RW_COLS) + [full2(a) for a in args],
        out_specs=[one, one, one, two, two, two, one, one],
        out_shape=[s1, s1, s1, s2, s2, s2, s1, s1],
        compiler_params=_cparams(("arbitrary", "arbitrary"), VMEM_LIMIT),
        name="rwkv_prep",
    )(rw_cols, rw_cols, rw_cols, *args)


def _chunk_cumsum(e, fwd):
    row = lax.broadcasted_iota(I32, e.shape, 0)
    c = e
    s = 1
    while s < CHUNK:
        if fwd:
            c = c + jnp.where(row >= s, pltpu.roll(c, s, 0), 0.0)
        else:
            c = c + jnp.where(row < CHUNK - s, pltpu.roll(c, CHUNK - s, 0), 0.0)
        s *= 2
    return c


def _block_diag(x4, lane_head):
    return jnp.concatenate([jnp.where(lane_head == h, x4, 0.0) for h in range(QUAD)], axis=0).astype(BF16)


def _mm(a, b):
    return jnp.dot(a.astype(BF16), b, preferred_element_type=F32)


def _mm_nt(a, b):
    return lax.dot_general(a.astype(BF16), b, (((1,), (1,)), ((), ())), preferred_element_type=F32)


def _scan_chain(fwd, r, v, kk, e, km, bv, s_ref, y_ref):
    n = CHUNK
    cum = _chunk_cumsum(e, fwd)
    tot = jnp.sum(e, axis=0, keepdims=True)
    grow = jnp.exp(cum)
    at = -kk * jnp.exp(e - cum)
    bt = bv * grow
    kt = km * grow
    rt = r * jnp.exp(-cum)
    tail = jnp.exp(cum - tot)
    bb = bv * tail
    kb = km * tail
    p_all = jnp.exp(-tot)

    row = lax.broadcasted_iota(I32, (n, QW), 0)
    lane = lax.broadcasted_iota(I32, (n, QW), 1)
    col = lane & (RW_HEAD - 1)
    lane_head = lane >> 6
    incl = (row >= col) if fwd else (row <= col)
    strict = (row > col) if fwd else (row < col)
    eye = jnp.where(row == col, 1.0, 0.0)
    bd = lambda x4: _block_diag(x4, lane_head)

    for q in range(NQ):
        ql = slice(QW * q, QW * (q + 1))
        a4, v4 = at[:, ql], v[:, ql]
        ar = jnp.concatenate([a4, rt[:, ql]], axis=0)
        bd_v = bd(v4)
        gb = _mm_nt(ar, bd(bt[:, ql]))
        gk = _mm_nt(ar, bd(kt[:, ql]))
        m = jnp.where(strict, gb[:n], 0.0)
        g_rb = jnp.where(incl, gb[n:], 0.0)
        g_rk = jnp.where(incl, gk[n:], 0.0)
        w4 = _mm(jnp.where(strict, gk[:n], 0.0), bd_v)
        t4 = eye + m
        p = _mm(m, bd(m))
        for level in range(1, 6):
            bd_p = bd(p)
            if level < 5:
                both = _mm(jnp.concatenate([p, t4], axis=0), bd_p)
                p = both[:n]
                t4 = t4 + both[n:]
            else:
                t4 = t4 + _mm(t4, bd_p)
        s4 = s_ref[q]
        sa = _mm_nt(ar, bd(s4))
        u4 = _mm(t4, bd(sa[:n] + w4))
        y_ref[:, ql] = sa[n:] + _mm(g_rb, bd(u4)) + _mm(g_rk, bd_v)
        uv = jnp.concatenate([u4, v4], axis=0).astype(BF16)
        bk = jnp.concatenate([bb[:, ql], kb[:, ql]], axis=0).astype(BF16)
        full = lax.dot_general(uv, bk, (((0,), (0,)), ((), ())), preferred_element_type=F32)
        upd = s4 * p_all[:, ql]
        for h in range(QUAD):
            upd = upd + jnp.where(lane_head == h, full[RW_HEAD * h:RW_HEAD * (h + 1)], 0.0)
        s_ref[q] = upd


def _rw_scan_body(nb, rf_ref, vf_ref, kf_ref, rb_ref, vb_ref, kb_ref, ef_ref, mf_ref, bf_ref, eb_ref, mb_ref, bb_ref,
                  s0_ref, yf_ref, yb_ref, sf_ref, s_scr):
    c = pl.program_id(1)

    @pl.when(c == 0)
    def _():
        s_scr[...] = s0_ref[...]

    for i in range(nb):
        _scan_chain(True, rf_ref[i], vf_ref[i], kf_ref[i], ef_ref[i], mf_ref[i], bf_ref[i], s_scr.at[0, i], yf_ref.at[i])
        _scan_chain(False, rb_ref[i], vb_ref[i], kb_ref[i], eb_ref[i], mb_ref[i], bb_ref[i], s_scr.at[1, i],
                    yb_ref.at[i])

    @pl.when(c == pl.num_programs(1) - 1)
    def _():
        sf_ref[...] = s_scr[...]


def _rwkv_scan(r, v, kk, e, km, bvec, s0):
    bx, tx, _ = r.shape
    nc = tx // CHUNK
    nb = 2
    fpos = lambda g, c: (g, c, 0)
    bpos = lambda g, c: (g, nc - 1 - c, 0)
    tok = lambda f: pl.BlockSpec((nb, CHUNK, RW_WIDTH), f)
    tokd = lambda d, f: pl.BlockSpec((None, nb, CHUNK, RW_WIDTH), lambda g, c: (d,) + f(g, c))
    st = pl.BlockSpec((2, nb, NQ, RW_HEAD, QW), lambda g, c: (0, g, 0, 0, 0))
    y_shape = jax.ShapeDtypeStruct((bx, tx, RW_WIDTH), F32)
    return pl.pallas_call(
        functools.partial(_rw_scan_body, nb),
        grid=(bx // nb, nc),
        in_specs=[tok(fpos)] * 3 + [tok(bpos)] * 3 + [tokd(0, fpos)] * 3 + [tokd(1, bpos)] * 3 + [st],
        out_specs=[tok(fpos), tok(bpos), st],
        out_shape=[y_shape, y_shape, jax.ShapeDtypeStruct((2, bx, NQ, RW_HEAD, QW), F32)],
        scratch_shapes=[pltpu.VMEM((2, nb, NQ, RW_HEAD, QW), F32)],
        compiler_params=_cparams(("arbitrary", "arbitrary"), VMEM_LIMIT),
        name="rwkv_scan",
    )(r, v, kk, r, v, kk, e, km, bvec, e, km, bvec, s0)


def _merge_body(hy_ref, yf_ref, yb_ref, bonus_ref, g_ref, gate_ref, x_ref, hyp_ref, rwp_ref, wo_ref, ones_ref,
                lng_ref, lnb_ref, m2_ref, n1_ref, n2_ref, sc4_ref, m3_ref, rwh_ref, rwl_ref,
                x1_ref, h2_ref, aff_ref, afft_ref):
    ones = ones_ref[...]
    ysum = yf_ref[...] + yb_ref[...]
    mean = _head_sums(ysum, ones) * (1.0 / RW_HEAD)
    dev = ysum - mean
    var = _head_sums(dev * dev, ones) * (1.0 / RW_HEAD)
    yn = dev * lax.rsqrt(var + RW_LN_EPS) * lng_ref[...] + lnb_ref[...]
    y_rw = (yn + bonus_ref[...]) * g_ref[...]
    gates = _sigmoid(gate_ref[...])
    mix = (gates[:, :D_MODEL] * _bdot(hy_ref[...].astype(BF16), hyp_ref[...])
           + gates[:, D_MODEL:] * _bdot(y_rw.astype(BF16), rwp_ref[...]))
    mixed = _bdot(mix.astype(BF16), wo_ref[...])
    x1 = x_ref[...] + m2_ref[...] * (_rms(mixed) * n1_ref[...])
    x1_ref[...] = x1
    h2 = _rms(x1) * n2_ref[...] * sc4_ref[...] + m3_ref[...]
    h2_ref[...] = h2
    logits = _dot_w3(h2, rwh_ref[...], rwl_ref[...])
    lane = lax.broadcasted_iota(I32, logits.shape, 1)
    logits = jnp.where(lane < N_EXPERTS, logits, -1e30)
    ex = jnp.exp(logits - jnp.max(logits, axis=1, keepdims=True))
    aff = ex / jnp.sum(ex, axis=1, keepdims=True)
    aff_ref[...] = aff
    afft_ref[...] = jnp.transpose(aff)[:N_EXPERTS, :]


def _merge(out_hy, y_f, y_b, bonus, g, gate_cols, x, hy_proj, rw_proj, w_out, ln_g, ln_b, m2, n1post, n2pre, sc4, m3,
           router_w):
    tt = 256
    tok = lambda w: pl.BlockSpec((None, tt, w), lambda b, t: (b, t, 0))
    full = lambda a: pl.BlockSpec(a.shape, lambda b, t: (0,) * a.ndim)
    per_b = pl.BlockSpec((None, 1, D_MODEL), lambda b, t: (b, 0, 0))
    vec = lambda a, n: a.reshape(1, n)
    consts = (hy_proj.astype(BF16), rw_proj.astype(BF16), w_out.astype(BF16), _head_ones(),
              vec(ln_g, RW_WIDTH), vec(ln_b, RW_WIDTH))
    router = _split(_pad2(router_w, D_MODEL, 128))
    return pl.pallas_call(
        _merge_body,
        grid=(BATCH, SEQ // tt),
        in_specs=[tok(HY_WIDTH), tok(RW_WIDTH), tok(RW_WIDTH), tok(RW_WIDTH), tok(RW_WIDTH), tok(GATE_COLS),
                  tok(D_MODEL)]
                 + [full(a) for a in consts]
                 + [per_b, full(vec(n1post, D_MODEL)), full(vec(n2pre, D_MODEL)), per_b, per_b,
                    full(router[0]), full(router[1])],
        out_specs=[tok(D_MODEL), tok(D_MODEL), tok(128),
                   pl.BlockSpec((None, N_EXPERTS, tt), lambda b, t: (b, 0, t))],
        out_shape=[jax.ShapeDtypeStruct((BATCH, SEQ, D_MODEL), F32),
                   jax.ShapeDtypeStruct((BATCH, SEQ, D_MODEL), F32),
                   jax.ShapeDtypeStruct((BATCH, SEQ, 128), F32),
                   jax.ShapeDtypeStruct((BATCH, N_EXPERTS, SEQ), F32)],
        compiler_params=_cparams(("arbitrary", "arbitrary"), VMEM_LIMIT),
        name="merge_router",
    )(out_hy, y_f, y_b, bonus, g, gate_cols, x, *consts, m2, vec(n1post, D_MODEL), vec(n2pre, D_MODEL), sc4, m3,
      *router)


PAIR = 256


def _topk_body(afft_ref, aff_ref, out_ref, key_scr):
    ne = N_EXPERTS

    def bisect(_, carry):
        lo, hi = carry
        mid = lo + ((hi - lo) >> 1)
        n_ge = jnp.sum(jnp.where(afft_ref[...] >= pltpu.bitcast(mid, F32), 1.0, 0.0), axis=1, keepdims=True)
        ok = n_ge >= float(CAP)
        return jnp.where(ok, mid, lo), jnp.where(ok, hi, mid)

    lo0 = jnp.zeros((ne, 1), I32)
    hi0 = jnp.full((ne, 1), 0x7F800000, I32)
    thr_bits, _ = lax.fori_loop(0, 32, bisect, (lo0, hi0))
    thr = pltpu.bitcast(thr_bits, F32)
    need = float(CAP) - jnp.sum(jnp.where(afft_ref[...] > thr, 1.0, 0.0), axis=1, keepdims=True)

    ri = lax.broadcasted_iota(I32, (128, 128), 0)
    ci = lax.broadcasted_iota(I32, (128, 128), 1)
    upper = jnp.where(ri <= ci, 1.0, 0.0).astype(BF16)
    off_eq = jnp.zeros((ne, 1), F32)
    off_sel = jnp.zeros((ne, 1), F32)
    for blk in range(SEQ // 128):
        sl = slice(128 * blk, 128 * (blk + 1))
        a_b = afft_ref[:, sl]
        eq_b = jnp.where(a_b == thr, 1.0, 0.0)
        inc = _bdot(eq_b.astype(BF16), upper) + off_eq
        off_eq = inc[:, 127:128]
        sel_b = jnp.where(a_b > thr, 1.0, 0.0) + eq_b * jnp.where(inc - eq_b < need, 1.0, 0.0)
        cnt = _bdot(sel_b.astype(BF16), upper) + off_sel
        off_sel = cnt[:, 127:128]
        key_scr[:, sl] = cnt * sel_b

    out_ref[...] = jnp.zeros_like(out_ref)
    slot1 = lax.broadcasted_iota(I32, (CAP, PAIR), 0).astype(F32) + 1.0
    lane = lax.broadcasted_iota(I32, (PAIR, 128), 1)
    pos = lax.broadcasted_iota(I32, (PAIR, 128), 0).astype(F32)
    for ex in range(ne):
        def per_pair(tp, carry, ex=ex):
            o = pl.multiple_of(tp * PAIR, PAIR)
            onehot = jnp.where(key_scr[ex:ex + 1, pl.ds(o, PAIR)] == slot1, 1.0, 0.0).astype(BF16)
            a0 = aff_ref[pl.ds(o, PAIR), ex:ex + 1]
            a_hi = a0.astype(BF16).astype(F32)
            a1 = a0 - a_hi
            a_mid = a1.astype(BF16).astype(F32)
            a_lo = a1 - a_mid
            pair = (tp + jnp.zeros((PAIR, 128), I32)).astype(F32)
            payload = jnp.where(lane == ex, pos, jnp.where(lane == ne + ex, pair, jnp.where(
                lane == 2 * ne + ex, a_hi, jnp.where(lane == 3 * ne + ex, a_mid,
                                                     jnp.where(lane == 4 * ne + ex, a_lo, 0.0)))))
            out_ref[...] += _bdot(onehot, payload.astype(BF16))
            return carry

        lax.fori_loop(0, SEQ // PAIR, per_pair, 0)


def _expert_choice(aff_t, aff):
    return pl.pallas_call(
        _topk_body,
        grid=(BATCH,),
        in_specs=[pl.BlockSpec((None, N_EXPERTS, SEQ), lambda b: (b, 0, 0)),
                  pl.BlockSpec((None, SEQ, 128), lambda b: (b, 0, 0))],
        out_specs=pl.BlockSpec((None, CAP, 128), lambda b: (b, 0, 0)),
        out_shape=jax.ShapeDtypeStruct((BATCH, CAP, 128), F32),
        scratch_shapes=[pltpu.VMEM((N_EXPERTS, SEQ), F32)],
        compiler_params=_cparams(("arbitrary",), VMEM_LIMIT),
        name="expert_choice",
    )(aff_t, aff)


HALF = D_MODEL // 2


def _gather_body(idx_ref, h_ref, o_ref, buf):
    sub = lax.broadcasted_iota(I32, (8, HALF), 0)

    def group(gi, carry):
        base = pl.multiple_of(gi * 8, 8)
        acc = jnp.zeros((8, HALF), F32)
        for j in range(8):
            row = idx_ref[0, base + j]
            tile = h_ref[pl.ds(pl.multiple_of((row >> 3) << 3, 8), 8), :]
            acc = jnp.where(sub == j, pltpu.roll(tile, (j - row) & 7, 0), acc)
        buf[pl.ds(base, 8), :] = acc
        return carry

    lax.fori_loop(0, CAP // 8, group, 0)
    o_ref[...] = buf[...].astype(BF16)


def _gather(idx3, h2):
    return pl.pallas_call(
        _gather_body,
        grid=(BATCH, 2, N_EXPERTS),
        in_specs=[pl.BlockSpec((None, 1, CAP), lambda b, dh, e: (b * N_EXPERTS + e, 0, 0),
                               memory_space=pltpu.SMEM),
                  pl.BlockSpec((None, SEQ, HALF), lambda b, dh, e: (b, 0, dh))],
        out_specs=pl.BlockSpec((None, None, CAP, HALF), lambda b, dh, e: (b, e, 0, dh)),
        out_shape=jax.ShapeDtypeStruct((BATCH, N_EXPERTS, CAP, D_MODEL), BF16),
        scratch_shapes=[pltpu.VMEM((CAP, HALF), F32)],
        compiler_params=_cparams(("arbitrary", "arbitrary", "arbitrary"), VMEM_LIMIT),
        name="moe_gather",
    )(idx3, h2)


def _ffn_body(x_ref, w1_ref, w3_ref, w2_ref, gt_ref, o_ref):
    f = pl.program_id(2)
    x = x_ref[...]
    a = jnp.dot(x, w1_ref[...].astype(BF16), preferred_element_type=F32)
    b = jnp.dot(x, w3_ref[...].astype(BF16), preferred_element_type=F32)
    hid = (a * _sigmoid(a) * b).astype(BF16)
    part = jnp.dot(hid, w2_ref[...].astype(BF16), preferred_element_type=F32)

    @pl.when(f == 0)
    def _():
        o_ref[...] = part

    @pl.when(f > 0)
    def _():
        o_ref[...] += part

    @pl.when(f == pl.num_programs(2) - 1)
    def _():
        o_ref[...] = o_ref[...] * gt_ref[...]


def _experts(xe, w1, w3, w2, gates4):
    tf = 512
    return pl.pallas_call(
        _ffn_body,
        grid=(BATCH, N_EXPERTS, D_EXPERT // tf),
        in_specs=[pl.BlockSpec((None, None, CAP, D_MODEL), lambda b, e, f: (b, e, 0, 0)),
                  pl.BlockSpec((None, D_MODEL, tf), lambda b, e, f: (e, 0, f)),
                  pl.BlockSpec((None, D_MODEL, tf), lambda b, e, f: (e, 0, f)),
                  pl.BlockSpec((None, tf, D_MODEL), lambda b, e, f: (e, f, 0)),
                  pl.BlockSpec((None, None, CAP, 1), lambda b, e, f: (b, e, 0, 0))],
        out_specs=pl.BlockSpec((None, None, CAP, D_MODEL), lambda b, e, f: (b, e, 0, 0)),
        out_shape=jax.ShapeDtypeStruct((BATCH, N_EXPERTS, CAP, D_MODEL), F32),
        compiler_params=_cparams(("arbitrary", "arbitrary", "arbitrary"), VMEM_LIMIT),
        name="moe_experts",
    )(xe, w1, w3, w2, gates4)


def _scatter_body(idx_ref, ye_ref, o_ref):
    @pl.when(pl.program_id(2) == 0)
    def _():
        o_ref[...] = jnp.zeros_like(o_ref)

    sub = lax.broadcasted_iota(I32, (8, HALF), 0)

    def group(gi, carry):
        base = pl.multiple_of(gi * 8, 8)
        rows8 = ye_ref[pl.ds(base, 8), :]
        for j in range(8):
            row = idx_ref[0, base + j]
            tile = pl.ds(pl.multiple_of((row >> 3) << 3, 8), 8)
            add = jnp.where(sub == (row & 7), jnp.broadcast_to(rows8[j:j + 1, :], (8, HALF)), 0.0)
            o_ref[tile, :] = o_ref[tile, :] + add
        return carry

    lax.fori_loop(0, CAP // 8, group, 0)


def _scatter(idx3, ye):
    return pl.pallas_call(
        _scatter_body,
        grid=(BATCH, 2, N_EXPERTS),
        in_specs=[pl.BlockSpec((None, 1, CAP), lambda b, dh, e: (b * N_EXPERTS + e, 0, 0),
                               memory_space=pltpu.SMEM),
                  pl.BlockSpec((None, None, CAP, HALF), lambda b, dh, e: (b, e, 0, dh))],
        out_specs=pl.BlockSpec((None, SEQ, HALF), lambda b, dh, e: (b, 0, dh)),
        out_shape=jax.ShapeDtypeStruct((BATCH, SEQ, D_MODEL), F32),
        compiler_params=_cparams(("arbitrary", "arbitrary", "arbitrary"), VMEM_LIMIT),
        name="moe_scatter",
    )(idx3, ye)


def _final_body(x1_ref, moe_ref, m5_ref, n_ref, o_ref):
    o_ref[...] = x1_ref[...] + m5_ref[...] * (_rms(moe_ref[...]) * n_ref[...])


def _final(x1, moe, m5, n2post):
    tt = 512
    tok = pl.BlockSpec((None, tt, D_MODEL), lambda b, t: (b, t, 0))
    return pl.pallas_call(
        _final_body,
        grid=(BATCH, SEQ // tt),
        in_specs=[tok, tok, pl.BlockSpec((None, 1, D_MODEL), lambda b, t: (b, 0, 0)),
                  pl.BlockSpec((1, D_MODEL), lambda b, t: (0, 0))],
        out_specs=tok,
        out_shape=jax.ShapeDtypeStruct((BATCH, SEQ, D_MODEL), F32),
        compiler_params=_cparams(("arbitrary", "arbitrary")),
        name="moe_residual",
    )(x1, moe, m5, n2post.reshape(1, D_MODEL))


def _layer(x, c, ctx, c_ctx, p):
    c8 = jnp.zeros((8, D_MODEL), F32).at[:BATCH].set(c).at[BATCH].set(c_ctx)
    mod = _modulation(c8, p['mod_w'], p['mod_b'])
    m_lat = [mod[:BATCH, i * D_MODEL:(i + 1) * D_MODEL].reshape(BATCH, 1, D_MODEL) for i in range(6)]
    m_ctx = [jnp.broadcast_to(mod[BATCH, i * D_MODEL:(i + 1) * D_MODEL], (BATCH, 1, D_MODEL)) for i in range(2)]

    w_in = p['w_in'].astype(BF16)
    w_hy = w_in[:, :HY_COLS]
    w_rw = w_in[:, HY_COLS:HY_COLS + RW_COLS]
    w_gate = w_in[:, HY_COLS + RW_COLS:]
    rw_args = (p['rw_mu'], p['rw_w_up'], p['rw_a_up'], p['rw_w0'], p['rw_a0'], p['rw_g_up'],
               p['rw_k_k'], p['rw_k_a'], p['rw_r_k'])

    (ctx_cols,) = _in_projection(ctx, p['norm1_pre'], 1.0 + m_ctx[1], m_ctx[0], [w_rw], CTX_LEN, 1)
    cr, cv, ckk, ce, ckm, cb, _, _ = _rwkv_prep(ctx_cols, *rw_args)
    zero_state = jnp.zeros((2, BATCH, NQ, RW_HEAD, QW), F32)
    _, _, ctx_states = _rwkv_scan(cr, cv, ckk, ce, ckm, cb, zero_state)

    hy_cols, rw_cols, gate_cols = _in_projection(x, p['norm1_pre'], 1.0 + m_lat[1], m_lat[0],
                                                 [w_hy, w_rw, w_gate], 512, 2)
    out_hy = _hyena_branch(hy_cols, p['hy_conv_w'], p['hy_conv_b'], p['hy_ffn_w1'], p['hy_ffn_b1'],
                           p['hy_ffn_w2'], p['hy_ffn_b2'], p['hy_ffn_w3'], p['hy_freq'], p['hy_bias'])
    r, v, kk, e, km, bvec, bonus, g = _rwkv_prep(rw_cols, *rw_args)
    y_f, y_b, _ = _rwkv_scan(r, v, kk, e, km, bvec, ctx_states)

    x1, h2, aff, aff_t = _merge(out_hy, y_f, y_b, bonus, g, gate_cols, x, p['hy_proj'], p['rw_proj'], p['w_out'],
                                p['rw_ln_g'], p['rw_ln_b'], m_lat[2], p['norm1_post'], p['norm2_pre'],
                                1.0 + m_lat[4], m_lat[3], p['router_w'])
    picked = _expert_choice(aff_t, aff)
    ne = N_EXPERTS
    idx = (PAIR * picked[:, :, ne:2 * ne] + picked[:, :, :ne]).astype(I32)
    gates = (picked[:, :, 2 * ne:3 * ne] + picked[:, :, 3 * ne:4 * ne]) + picked[:, :, 4 * ne:5 * ne]
    idx3 = jnp.swapaxes(idx, 1, 2).reshape(BATCH * ne, 1, CAP)
    gates4 = jnp.swapaxes(gates, 1, 2).reshape(BATCH, ne, CAP, 1)
    xe = _gather(idx3, h2)
    ye = _experts(xe, p['exp_w1'], p['exp_w3'], p['exp_w2'], gates4)
    moe = _scatter(idx3, ye)
    return _final(x1, moe, m_lat[5], p['norm2_post'])


def kernel(x, c, ctx, c_ctx, mod_w, mod_b, norm1_pre, norm1_post, norm2_pre, norm2_post, w_in, hy_conv_w, hy_conv_b, hy_ffn_w1, hy_ffn_b1, hy_ffn_w2, hy_ffn_b2, hy_ffn_w3, hy_freq, hy_bias, hy_proj, rw_mu, rw_w0, rw_w_up, rw_a0, rw_a_up, rw_g_up, rw_k_k, rw_k_a, rw_r_k, rw_ln_g, rw_ln_b, rw_proj, w_out, router_w, exp_w1, exp_w3, exp_w2):
    names = ('mod_w', 'mod_b', 'norm1_pre', 'norm1_post', 'norm2_pre', 'norm2_post', 'w_in', 'hy_conv_w',
             'hy_conv_b', 'hy_ffn_w1', 'hy_ffn_b1', 'hy_ffn_w2', 'hy_ffn_b2', 'hy_ffn_w3', 'hy_freq', 'hy_bias',
             'hy_proj', 'rw_mu', 'rw_w0', 'rw_w_up', 'rw_a0', 'rw_a_up', 'rw_g_up', 'rw_k_k', 'rw_k_a', 'rw_r_k',
             'rw_ln_g', 'rw_ln_b', 'rw_proj', 'w_out', 'router_w', 'exp_w1', 'exp_w3', 'exp_w2')
    vals = (mod_w, mod_b, norm1_pre, norm1_post, norm2_pre, norm2_post, w_in, hy_conv_w, hy_conv_b, hy_ffn_w1,
            hy_ffn_b1, hy_ffn_w2, hy_ffn_b2, hy_ffn_w3, hy_freq, hy_bias, hy_proj, rw_mu, rw_w0, rw_w_up, rw_a0,
            rw_a_up, rw_g_up, rw_k_k, rw_k_a, rw_r_k, rw_ln_g, rw_ln_b, rw_proj, w_out, router_w, exp_w1, exp_w3,
            exp_w2)
    depth = mod_w.shape[0]
    assert depth == 1, "single-layer block: the context stream only provides scan start states"
    p = {n: a[0] for n, a in zip(names, vals)}
    return _layer(x, c, ctx, c_ctx, p)
```

```python
import functools
import math

import numpy as np
import jax
import jax.numpy as jnp
from jax import lax
from jax.experimental import pallas as pl
from jax.experimental.pallas import tpu as pltpu

F32 = jnp.float32
BF16 = jnp.bfloat16
I32 = jnp.int32
HI = lax.Precision.HIGHEST

D_MODEL = 1024
BATCH = 4
SEQ = 8192
CTX_LEN = 256

HY_WIDTH = 512
HY_BANDS = 16
HY_FILTER_ORDER = 64
HY_DECAY_TARGET = 1e-2
HY_DECAY_SHORT = 0.3
HY_DECAY_LONG = 1.5

RW_HEADS = 8
RW_HEAD = 64
RW_WIDTH = RW_HEADS * RW_HEAD
LORA_W = 64
LORA_A = 64
LORA_G = 128
RW_LN_EPS = 64e-5

HY_COLS = 3 * HY_WIDTH
RW_COLS = 3 * RW_WIDTH + LORA_W + LORA_A + LORA_G
GATE_COLS = 2 * D_MODEL

N_EXPERTS = 16
D_EXPERT = 2048
EC_CAPACITY = 2
NORM_EPS = 1e-6
CAP = EC_CAPACITY * SEQ // N_EXPERTS

NFFT = 2 * SEQ
FFT_R = 128
FFT_H = FFT_R // 2
FFT_J = 8
FFT_C = 512
CHUNK = 64
QUAD = 4
QW = QUAD * RW_HEAD
NQ = RW_HEADS // QUAD

VMEM_LIMIT = 56 * 1024 * 1024


def _cparams(sem, vmem=None, flags=None):
    return pltpu.CompilerParams(dimension_semantics=sem, vmem_limit_bytes=vmem, flags=flags)


def _dot_hi(a, b):
    return jnp.dot(a, b, precision=HI, preferred_element_type=F32)


def _bdot(a, b):
    return jnp.dot(a, b, preferred_element_type=F32)


def _split(a):
    hi = a.astype(BF16)
    return hi, (a - hi.astype(F32)).astype(BF16)


def _dot_c(m, x):
    return _bdot(m, x.astype(BF16))


def _dot_w3(x, w_hi, w_lo):
    xh, xl = _split(x)
    return _bdot(xh, w_hi) + (_bdot(xl, w_hi) + _bdot(xh, w_lo))


def _head_sums(x, ones_bf):
    xh, xl = _split(x)
    return _bdot(xh, ones_bf) + _bdot(xl, ones_bf)


def _rms(x):
    return x * lax.rsqrt(jnp.mean(x * x, axis=-1, keepdims=True) + NORM_EPS)


def _sigmoid(x):
    return 1.0 / (1.0 + jnp.exp(-x))


def _pad2(a, rows, cols):
    return jnp.zeros((rows, cols), F32).at[:a.shape[0], :a.shape[1]].set(a)


def _mod_body(c_ref, w_ref, b_ref, o_ref):
    c = c_ref[...]
    o_ref[...] = _dot_hi(c * _sigmoid(c), w_ref[...]) + b_ref[...]


def _modulation(c8, mod_w, mod_b):
    n = mod_w.shape[1]
    tn = 1536
    return pl.pallas_call(
        _mod_body,
        grid=(n // tn,),
        in_specs=[pl.BlockSpec((8, D_MODEL), lambda j: (0, 0)),
                  pl.BlockSpec((D_MODEL, tn), lambda j: (0, j)),
                  pl.BlockSpec((1, tn), lambda j: (0, j))],
        out_specs=pl.BlockSpec((8, tn), lambda j: (0, j)),
        out_shape=jax.ShapeDtypeStruct((8, n), F32),
        compiler_params=_cparams(("arbitrary",)),
        name="modulation",
    )(c8, mod_w, mod_b.reshape(1, n))


def _inproj_body(nw, x_ref, g_ref, sc_ref, sh_ref, *refs):
    w_refs, o_refs, h_ref = refs[:nw], refs[nw:2 * nw], refs[2 * nw]

    @pl.when(pl.program_id(2) == 0)
    def _():
        h = _rms(x_ref[...]) * g_ref[...] * sc_ref[...] + sh_ref[...]
        h_ref[...] = h.astype(BF16)

    h = h_ref[...]
    for w_ref, o_ref in zip(w_refs, o_refs):
        o_ref[...] = jnp.dot(h, w_ref[...], preferred_element_type=F32).astype(o_ref.dtype)


def _in_projection(x, g, scale1p, shift, weights, tm, nsplit, out_dtypes=None):
    out_dtypes = out_dtypes or [F32] * len(weights)
    bx, tx, _ = x.shape
    nw = len(weights)
    tns = [w.shape[1] // nsplit for w in weights]
    in_specs = [pl.BlockSpec((None, tm, D_MODEL), lambda b, m, n: (b, m, 0)),
                pl.BlockSpec((1, D_MODEL), lambda b, m, n: (0, 0)),
                pl.BlockSpec((None, 1, D_MODEL), lambda b, m, n: (b, 0, 0)),
                pl.BlockSpec((None, 1, D_MODEL), lambda b, m, n: (b, 0, 0))]
    in_specs += [pl.BlockSpec((D_MODEL, tn), lambda b, m, n: (0, n)) for tn in tns]
    out_specs = [pl.BlockSpec((None, tm, tn), lambda b, m, n: (b, m, n)) for tn in tns]
    out_shape = [jax.ShapeDtypeStruct((bx, tx, w.shape[1]), dt) for w, dt in zip(weights, out_dtypes)]
    return pl.pallas_call(
        functools.partial(_inproj_body, nw),
        grid=(bx, tx // tm, nsplit),
        in_specs=in_specs,
        out_specs=out_specs,
        out_shape=out_shape,
        scratch_shapes=[pltpu.VMEM((tm, D_MODEL), BF16)],
        compiler_params=_cparams(("arbitrary", "arbitrary", "arbitrary"), VMEM_LIMIT),
        name="in_projection",
    )(x, g.reshape(1, D_MODEL), scale1p, shift, *weights)


def _neighbours(cur, pv_ref, nx_ref):
    tt = cur.shape[0]
    t = pl.program_id(1)
    first = (t > 0).astype(F32)
    last = (t < pl.num_programs(1) - 1).astype(F32)
    prev_row = pv_ref[7:8, :] * first
    next_row = nx_ref[0:1, :] * last
    rows = lax.broadcasted_iota(I32, cur.shape, 0)
    prev = jnp.where(rows == 0, prev_row, pltpu.roll(cur, 1, 0))
    nxt = jnp.where(rows == tt - 1, next_row, pltpu.roll(cur, tt - 1, 0))
    return prev, nxt


def _halo_specs(tt, t_total, width):
    r8 = tt // 8
    last8 = t_total // 8 - 1
    return [pl.BlockSpec((None, tt, width), lambda b, t: (b, t, 0)),
            pl.BlockSpec((None, 8, width), lambda b, t: (b, jnp.maximum(t * r8 - 1, 0), 0)),
            pl.BlockSpec((None, 8, width), lambda b, t: (b, jnp.minimum((t + 1) * r8, last8), 0))]


def _hy_pre_body(cur_ref, pv_ref, nx_ref, w_ref, b_ref, z_ref, x0_ref):
    cur = cur_ref[...]
    prev, nxt = _neighbours(cur, pv_ref, nx_ref)
    u = w_ref[0:1, :] * prev + w_ref[1:2, :] * cur + w_ref[2:3, :] * nxt + b_ref[...]
    z_ref[...] = u[:, 2 * HY_WIDTH:] * u[:, HY_WIDTH:2 * HY_WIDTH]
    x0_ref[...] = u[:, :HY_WIDTH]


def _hyena_pre(hy_cols, conv_w, conv_b):
    tt = 512
    spec_o = pl.BlockSpec((None, tt, HY_WIDTH), lambda b, t: (b, t, 0))
    return pl.pallas_call(
        _hy_pre_body,
        grid=(BATCH, SEQ // tt),
        in_specs=_halo_specs(tt, SEQ, HY_COLS) + [
            pl.BlockSpec((3, HY_COLS), lambda b, t: (0, 0)),
            pl.BlockSpec((1, HY_COLS), lambda b, t: (0, 0))],
        out_specs=[spec_o, spec_o],
        out_shape=[jax.ShapeDtypeStruct((BATCH, SEQ, HY_WIDTH), F32)] * 2,
        compiler_params=_cparams(("arbitrary", "arbitrary"), VMEM_LIMIT),
        name="hyena_pre",
    )(hy_cols, hy_cols, hy_cols, conv_w, conv_b.reshape(1, HY_COLS))


def _filt_body(bands_ref, w1t_ref, w1c_ref, w1s_ref, b1_ref, w2_ref, b2_ref, w3_ref, fr_ref, dl_ref,
               f_ref, s_ref):
    i = pl.program_id(0)
    tt = f_ref.shape[0]
    rowi = lax.broadcasted_iota(I32, (tt, 1), 0) + i * tt
    rowf = rowi.astype(F32)
    tpos = rowf * (1.0 / (SEQ - 1))
    arg = (rowf * (2.0 * math.pi / SEQ)) * bands_ref[...]
    pre = tpos * w1t_ref[...] + _dot_hi(jnp.cos(arg), w1c_ref[...]) - _dot_hi(jnp.sin(arg), w1s_ref[...]) + b1_ref[...]
    freq = fr_ref[...]
    hid = jnp.sin(freq * pre)
    hid = jnp.sin(freq * (_dot_hi(hid, w2_ref[...]) + b2_ref[...]))
    filt = _dot_hi(hid, w3_ref[...]) * jnp.exp(-tpos * dl_ref[...])
    lane = lax.broadcasted_iota(I32, filt.shape, 1)
    rows = lax.broadcasted_iota(I32, filt.shape, 0) + i * tt
    filt = jnp.where((rows == 0) & (lane >= HY_WIDTH), 0.0, filt)
    f_ref[...] = filt

    @pl.when(i == 0)
    def _():
        s_ref[...] = jnp.zeros_like(s_ref)

    s_ref[...] += jnp.sum(jnp.abs(filt), axis=0, keepdims=True)


def _hyena_filter(w1, b1, w2, b2, w3, freq):
    tt = 512
    bands = np.zeros((1, 128), np.float32)
    bands[0, :HY_BANDS] = np.linspace(1e-4, HY_BANDS - 1, HY_BANDS, dtype=np.float32)
    deltas = np.abs(np.linspace(math.log(HY_DECAY_TARGET) / HY_DECAY_LONG,
                                math.log(HY_DECAY_TARGET) / HY_DECAY_SHORT, HY_WIDTH, dtype=np.float32))
    dl = np.concatenate([deltas, deltas])[None, :].astype(np.float32)
    args = (jnp.asarray(bands),
            _pad2(w1[0:1], 1, 128),
            _pad2(w1[1:1 + HY_BANDS], 128, 128),
            _pad2(w1[1 + HY_BANDS:], 128, 128),
            _pad2(b1[None], 1, 128),
            _pad2(w2, 128, 128),
            _pad2(b2[None], 1, 128),
            _pad2(w3, 128, 2 * HY_WIDTH),
            _pad2(freq[None], 1, 128),
            jnp.asarray(dl))
    full = lambda a: pl.BlockSpec(a.shape, lambda i: (0, 0))
    return pl.pallas_call(
        _filt_body,
        grid=(SEQ // tt,),
        in_specs=[full(a) for a in args],
        out_specs=[pl.BlockSpec((tt, 2 * HY_WIDTH), lambda i: (i, 0)),
                   pl.BlockSpec((1, 2 * HY_WIDTH), lambda i: (0, 0))],
        out_shape=[jax.ShapeDtypeStruct((SEQ, 2 * HY_WIDTH), F32),
                   jax.ShapeDtypeStruct((1, 2 * HY_WIDTH), F32)],
        compiler_params=_cparams(("arbitrary",)),
        name="hyena_filter",
    )(*args)


def _fft_tables():
    k = np.arange(FFT_R)
    n = np.arange(FFT_H)
    th = 2.0 * np.pi * np.outer(k, n) / FFT_R
    f1 = np.stack([np.cos(th), -np.sin(th)])
    eye = np.eye(FFT_J)
    g1 = np.einsum('pkn,jm->pkjnm', f1, eye).reshape(2 * FFT_R * FFT_J, FFT_H * FFT_J)
    f3 = np.stack([np.cos(th.T), -np.sin(th.T)]) / NFFT
    g3 = np.einsum('pnk,jm->njpkm', f3, eye).reshape(FFT_H * FFT_J, 2 * FFT_R * FFT_J)
    th2 = 2.0 * np.pi * np.outer(k, k) / FFT_R
    c, s = np.cos(th2), np.sin(th2)
    fc = np.block([[c, s], [-s, c]])
    fi = np.block([[c, -s], [s, c]])
    return tuple(jnp.asarray(a, F32) for a in (g1, g3, fc, fi))


def _twiddle(rows_k, n2):
    ph = ((rows_k * n2) & (NFFT - 1)).astype(F32) * (2.0 * math.pi / NFFT)
    return jnp.cos(ph), jnp.sin(ph)


def _fft1_body(z_ref, g_ref, o_ref):
    nb = pl.program_id(1)
    c = z_ref.shape[-1]
    z = z_ref[...].reshape(FFT_H * FFT_J, c)
    a = _dot_c(g_ref[...], z)
    half = FFT_R * FFT_J
    ar, ai = a[:half], a[half:]
    r = lax.broadcasted_iota(I32, (half, 1), 0)
    cs, sn = _twiddle(r >> (FFT_J.bit_length() - 1), nb * FFT_J + (r & (FFT_J - 1)))
    o_ref[0] = (ar * cs + ai * sn).reshape(FFT_R, FFT_J, c)
    o_ref[1] = (ai * cs - ar * sn).reshape(FFT_R, FFT_J, c)


def _fft_stage1(z4, g1):
    bz, _, _, c = z4.shape
    tab = pl.BlockSpec(g1.shape, lambda b, n, cb: (0, 0))
    return pl.pallas_call(
        _fft1_body,
        grid=(bz, FFT_R // FFT_J, c // FFT_C),
        in_specs=[pl.BlockSpec((None, FFT_H, FFT_J, FFT_C), lambda b, n, cb: (b, 0, n, cb)), tab],
        out_specs=pl.BlockSpec((None, 2, FFT_R, FFT_J, FFT_C), lambda b, n, cb: (b, 0, 0, n, cb)),
        out_shape=jax.ShapeDtypeStruct((bz, 2, FFT_R, FFT_R, c), F32),
        compiler_params=_cparams(("arbitrary", "arbitrary", "arbitrary"), VMEM_LIMIT),
        name="fft_stage1",
    )(z4, g1)


def _filt_spec_body(a_ref, f_ref, h_ref):
    x = _dot_c(f_ref[...], jnp.concatenate([a_ref[0], a_ref[1]], axis=0))
    xr, xi = x[:FFT_R], x[FFT_R:]
    h_ref[0] = xr[:, :HY_WIDTH] + xr[:, HY_WIDTH:]
    h_ref[1] = xi[:, :HY_WIDTH] - xi[:, HY_WIDTH:]


def _filter_spectrum(a, fc):
    tab = pl.BlockSpec(fc.shape, lambda k: (0, 0))
    return pl.pallas_call(
        _filt_spec_body,
        grid=(FFT_R,),
        in_specs=[pl.BlockSpec((None, 2, None, FFT_R, 2 * HY_WIDTH), lambda k: (0, 0, k, 0, 0)), tab],
        out_specs=pl.BlockSpec((None, 2, FFT_R, HY_WIDTH), lambda k: (k, 0, 0, 0)),
        out_shape=jax.ShapeDtypeStruct((FFT_R, 2, FFT_R, HY_WIDTH), F32),
        compiler_params=_cparams(("arbitrary",)),
        name="filter_spectrum",
    )(a, fc)


MID_K = 4


def _fft_mid_body(a_ref, h_ref, fc_ref, fi_ref, o_ref):
    for i in range(MID_K):
        k1 = pl.program_id(0) * MID_K + i
        x = _dot_c(fc_ref[...], jnp.concatenate([a_ref[0, i], a_ref[1, i]], axis=0))
        xr, xi = x[:FFT_R], x[FFT_R:]
        hr, him = h_ref[i, 0], h_ref[i, 1]
        yr = xr * hr - xi * him
        yi = xr * him + xi * hr
        p = _dot_c(fi_ref[...], jnp.concatenate([yr, yi], axis=0))
        pr, pim = p[:FFT_R], p[FFT_R:]
        cs, sn = _twiddle(lax.broadcasted_iota(I32, (FFT_R, 1), 0), k1)
        o_ref[0, i] = pr * cs - pim * sn
        o_ref[1, i] = pr * sn + pim * cs


def _fft_mid(a, h, fc, fi):
    bz = a.shape[0]
    blk = pl.BlockSpec((None, 2, MID_K, FFT_R, HY_WIDTH), lambda k, b: (b, 0, k, 0, 0))
    tab = pl.BlockSpec(fc.shape, lambda k, b: (0, 0))
    return pl.pallas_call(
        _fft_mid_body,
        grid=(FFT_R // MID_K, bz),
        in_specs=[blk, pl.BlockSpec((MID_K, 2, FFT_R, HY_WIDTH), lambda k, b: (k, 0, 0, 0)), tab, tab],
        out_specs=blk,
        out_shape=jax.ShapeDtypeStruct(a.shape, F32),
        compiler_params=_cparams(("arbitrary", "arbitrary"), VMEM_LIMIT),
        name="fft_mid",
    )(a, h, fc, fi)


def _ifft_body(q_ref, g_ref, z_ref, x0_ref, inv_ref, hb_ref, o_ref):
    c = HY_WIDTH
    q = q_ref[...].reshape(2 * FFT_R * FFT_J, c)
    conv = _dot_c(g_ref[...], q) * inv_ref[...]
    z = z_ref[...].reshape(FFT_H * FFT_J, c)
    x0 = x0_ref[...].reshape(FFT_H * FFT_J, c)
    o_ref[...] = ((conv + hb_ref[...] * z) * x0).reshape(FFT_H, FFT_J, c)


def _ifft_gate(q, g3, z4, x04, inv_norm, hy_bias):
    row = pl.BlockSpec((None, FFT_H, FFT_J, HY_WIDTH), lambda b, n: (b, 0, n, 0))
    vec = pl.BlockSpec((1, HY_WIDTH), lambda b, n: (0, 0))
    tab = pl.BlockSpec(g3.shape, lambda b, n: (0, 0))
    return pl.pallas_call(
        _ifft_body,
        grid=(BATCH, FFT_R // FFT_J),
        in_specs=[pl.BlockSpec((None, 2, FFT_R, FFT_J, HY_WIDTH), lambda b, n: (b, 0, 0, n, 0)),
                  tab, row, row, vec, vec],
        out_specs=row,
        out_shape=jax.ShapeDtypeStruct(z4.shape, F32),
        compiler_params=_cparams(("arbitrary", "arbitrary"), VMEM_LIMIT),
        name="ifft_gate",
    )(q, g3, z4, x04, inv_norm, hy_bias)


def _hyena_branch(hy_cols, conv_w, conv_b, w1, b1, w2, b2, w3, freq, hy_bias):
    g1, g3, fc, fi = (t.astype(BF16) for t in _fft_tables())
    z, x0 = _hyena_pre(hy_cols, conv_w, conv_b)
    filt, l1 = _hyena_filter(w1, b1, w2, b2, w3, freq)
    inv_norm = 1.0 / (l1[:, :HY_WIDTH] + l1[:, HY_WIDTH:])
    fa = _fft_stage1(filt.reshape(1, FFT_H, FFT_R, 2 * HY_WIDTH), g1)
    h = _filter_spectrum(fa, fc)
    z4 = z.reshape(BATCH, FFT_H, FFT_R, HY_WIDTH)
    za = _fft_stage1(z4, g1)
    q = _fft_mid(za, h, fc, fi)
    out = _ifft_gate(q, g3, z4, x0.reshape(z4.shape), inv_norm, hy_bias.reshape(1, HY_WIDTH))
    return out.reshape(BATCH, SEQ, HY_WIDTH)


def _rw_prep_body(cur_ref, pv_ref, nx_ref, mu_ref, w2h_ref, w2l_ref, w0_ref, a0_ref, guph_ref, gupl_ref, kkw_ref,
                  ka_ref, rk_ref, ones_ref, r_o, v_o, kk_o, e_o, km_o, b_o, bonus_o, g_o):
    cur = cur_ref[...]
    prev, nxt = _neighbours(cur, pv_ref, nx_ref)
    z = cur + mu_ref[0:1, :] * (prev - cur) + mu_ref[1:2, :] * (nxt - cur)
    w_ = RW_WIDTH
    r, k, v = z[:, :w_], z[:, w_:2 * w_], z[:, 2 * w_:3 * w_]
    ll = z[:, 3 * w_:3 * w_ + 128]
    lg = z[:, 3 * w_ + 128:]
    lane = lax.broadcasted_iota(I32, ll.shape, 1)
    tl = jnp.where(lane < LORA_W, jnp.tanh(ll), ll)
    ones = ones_ref[...]
    kkr = k * kkw_ref[...]
    kk = kkr * lax.rsqrt(jnp.maximum(_head_sums(kkr * kkr, ones), 1e-24))
    bonus = jnp.zeros_like(r)
    for d in range(2):
        proj = _dot_w3(tl, w2h_ref[d], w2l_ref[d])
        wl = proj[:, :w_] + w0_ref[d:d + 1, :]
        al = proj[:, w_:] + a0_ref[d:d + 1, :]
        nwl = -wl
        w = -(jnp.maximum(nwl, 0.0) + jnp.log(1.0 + jnp.exp(-jnp.abs(nwl)))) - 0.5
        a = _sigmoid(al)
        km = k * (1.0 + (a - 1.0) * ka_ref[...])
        e_o[d] = jnp.exp(w)
        km_o[d] = km
        b_o[d] = kk * a
        bonus = bonus + _head_sums(r * km * rk_ref[...], ones)
    r_o[...] = r
    v_o[...] = v.astype(BF16)
    kk_o[...] = kk
    bonus_o[...] = (bonus * v).astype(BF16)
    g_o[...] = _dot_w3(_sigmoid(lg), guph_ref[...], gupl_ref[...]).astype(BF16)


def _head_ones():
    h = np.arange(RW_WIDTH) // RW_HEAD
    return jnp.asarray((h[:, None] == h[None, :]).astype(np.float32), BF16)


def _rwkv_prep(rw_cols, mu, w_up, a_up, w0, a0, g_up, k_k, k_a, r_k):
    bx, tx, _ = rw_cols.shape
    tt = 256
    w2 = jnp.zeros((2, 128, 2 * RW_WIDTH), F32)
    w2 = w2.at[:, :LORA_W, :RW_WIDTH].set(w_up).at[:, LORA_W:, RW_WIDTH:].set(a_up)
    vec = lambda a: a.reshape(1, RW_WIDTH)
    full2 = lambda a: pl.BlockSpec(a.shape, lambda b, t: (0,) * a.ndim)
    args = (mu, *_split(w2), w0, a0, *_split(g_up), vec(k_k), vec(k_a), vec(r_k), _head_ones())
    one = pl.BlockSpec((None, tt, RW_WIDTH), lambda b, t: (b, t, 0))
    two = pl.BlockSpec((2, None, tt, RW_WIDTH), lambda b, t: (0, b, t, 0))
    s1 = jax.ShapeDtypeStruct((bx, tx, RW_WIDTH), F32)
    s1h = jax.ShapeDtypeStruct((bx, tx, RW_WIDTH), BF16)
    s2 = jax.ShapeDtypeStruct((2, bx, tx, RW_WIDTH), F32)
    return pl.pallas_call(
        _rw_prep_body,
        grid=(bx, tx // tt),
        in_specs=_halo_specs(tt, tx, RW_COLS) + [full2(a) for a in args],
        out_specs=[one, one, one, two, two, two, one, one],
        out_shape=[s1, s1h, s1, s2, s2, s2, s1h, s1h],
        compiler_params=_cparams(("arbitrary", "arbitrary"), VMEM_LIMIT),
        name="rwkv_prep",
    )(rw_cols, rw_cols, rw_cols, *args)


def _chunk_cumsum(e, fwd):
    row = lax.broadcasted_iota(I32, e.shape, 0)
    c = e
    s = 1
    while s < CHUNK:
        if fwd:
            c = c + jnp.where(row >= s, pltpu.roll(c, s, 0), 0.0)
        else:
            c = c + jnp.where(row < CHUNK - s, pltpu.roll(c, CHUNK - s, 0), 0.0)
        s *= 2
    return c


def _block_diag(x4, lane_head):
    return jnp.concatenate([jnp.where(lane_head == h, x4, 0.0) for h in range(QUAD)], axis=0).astype(BF16)


def _mm(a, b):
    return jnp.dot(a.astype(BF16), b, preferred_element_type=F32)


def _mm_nt(a, b):
    return lax.dot_general(a.astype(BF16), b, (((1,), (1,)), ((), ())), preferred_element_type=F32)


def _scan_chain(fwd, r, v, kk, e, km, bv, s_ref, y_ref):
    n = CHUNK
    cum = _chunk_cumsum(e, fwd)
    tot = jnp.sum(e, axis=0, keepdims=True)
    grow = jnp.exp(cum)
    at = -kk * jnp.exp(e - cum)
    bt = bv * grow
    kt = km * grow
    rt = r * jnp.exp(-cum)
    tail = jnp.exp(cum - tot)
    bb = bv * tail
    kb = km * tail
    p_all = jnp.exp(-tot)

    row = lax.broadcasted_iota(I32, (n, QW), 0)
    lane = lax.broadcasted_iota(I32, (n, QW), 1)
    col = lane & (RW_HEAD - 1)
    lane_head = lane >> 6
    incl = (row >= col) if fwd else (row <= col)
    strict = (row > col) if fwd else (row < col)
    eye = jnp.where(row == col, 1.0, 0.0)
    bd = lambda x4: _block_diag(x4, lane_head)

    def quad_steps(q):
        ql = slice(QW * q, QW * (q + 1))
        a4, v4 = at[:, ql], v[:, ql]
        ar = jnp.concatenate([a4, rt[:, ql]], axis=0)
        bd_v = bd(v4)
        gb = _mm_nt(ar, bd(bt[:, ql]))
        gk = _mm_nt(ar, bd(kt[:, ql]))
        yield
        m = jnp.where(strict, gb[:n], 0.0)
        g_rb = jnp.where(incl, gb[n:], 0.0)
        g_rk = jnp.where(incl, gk[n:], 0.0)
        w4 = _mm(jnp.where(strict, gk[:n], 0.0), bd_v)
        t4 = eye + m
        p = _mm(m, bd(m))
        s4 = s_ref[q]
        sa = _mm_nt(ar, bd(s4))
        yield
        for level in range(1, 6):
            bd_p = bd(p)
            if level < 5:
                both = _mm(jnp.concatenate([p, t4], axis=0), bd_p)
                p = both[:n]
                t4 = t4 + both[n:]
            else:
                t4 = t4 + _mm(t4, bd_p)
            yield
        u4 = _mm(t4, bd(sa[:n] + w4))
        yield
        y_ref[:, ql] = sa[n:] + _mm(g_rb, bd(u4)) + _mm(g_rk, bd_v)
        uv = jnp.concatenate([u4, v4], axis=0).astype(BF16)
        bk = jnp.concatenate([bb[:, ql], kb[:, ql]], axis=0).astype(BF16)
        full = lax.dot_general(uv, bk, (((0,), (0,)), ((), ())), preferred_element_type=F32)
        upd = s4 * p_all[:, ql]
        for h in range(QUAD):
            upd = upd + jnp.where(lane_head == h, full[RW_HEAD * h:RW_HEAD * (h + 1)], 0.0)
        s_ref[q] = upd

    return [quad_steps(q) for q in range(NQ)]


def _round_robin(gens):
    gens = list(gens)
    while gens:
        for g in list(gens):
            try:
                next(g)
            except StopIteration:
                gens.remove(g)


def _rw_scan_body(nb, rf_ref, vf_ref, kf_ref, rb_ref, vb_ref, kb_ref, ef_ref, mf_ref, bf_ref, eb_ref, mb_ref, bb_ref,
                  s0_ref, yf_ref, yb_ref, sf_ref, s_scr):
    c = pl.program_id(1)

    @pl.when(c == 0)
    def _():
        s_scr[...] = s0_ref[...]

    steps = []
    for i in range(nb):
        steps += _scan_chain(True, rf_ref[i], vf_ref[i].astype(F32), kf_ref[i], ef_ref[i], mf_ref[i], bf_ref[i],
                             s_scr.at[0, i], yf_ref.at[i])
        steps += _scan_chain(False, rb_ref[i], vb_ref[i].astype(F32), kb_ref[i], eb_ref[i], mb_ref[i], bb_ref[i],
                             s_scr.at[1, i], yb_ref.at[i])
    _round_robin(steps)

    @pl.when(c == pl.num_programs(1) - 1)
    def _():
        sf_ref[...] = s_scr[...]


def _rwkv_scan(r, v, kk, e, km, bvec, s0):
    bx, tx, _ = r.shape
    nc = tx // CHUNK
    nb = 4
    fpos = lambda g, c: (g, c, 0)
    bpos = lambda g, c: (g, nc - 1 - c, 0)
    tok = lambda f: pl.BlockSpec((nb, CHUNK, RW_WIDTH), f)
    tokd = lambda d, f: pl.BlockSpec((None, nb, CHUNK, RW_WIDTH), lambda g, c: (d,) + f(g, c))
    st = pl.BlockSpec((2, nb, NQ, RW_HEAD, QW), lambda g, c: (0, g, 0, 0, 0))
    y_shape = jax.ShapeDtypeStruct((bx, tx, RW_WIDTH), F32)
    return pl.pallas_call(
        functools.partial(_rw_scan_body, nb),
        grid=(bx // nb, nc),
        in_specs=[tok(fpos)] * 3 + [tok(bpos)] * 3 + [tokd(0, fpos)] * 3 + [tokd(1, bpos)] * 3 + [st],
        out_specs=[tok(fpos), tok(bpos), st],
        out_shape=[y_shape, y_shape, jax.ShapeDtypeStruct((2, bx, NQ, RW_HEAD, QW), F32)],
        scratch_shapes=[pltpu.VMEM((2, nb, NQ, RW_HEAD, QW), F32)],
        compiler_params=_cparams(("arbitrary", "arbitrary"), VMEM_LIMIT),
        name="rwkv_scan",
    )(r, v, kk, r, v, kk, e, km, bvec, e, km, bvec, s0)


def _merge_body(hy_ref, yf_ref, yb_ref, bonus_ref, g_ref, gate_ref, x_ref, hyp_ref, rwp_ref, wo_ref, ones_ref,
                lng_ref, lnb_ref, m2_ref, n1_ref, n2_ref, sc4_ref, m3_ref, rwh_ref, rwl_ref,
                x1_ref, h2_ref, aff_ref, afft_ref):
    ones = ones_ref[...]
    ysum = yf_ref[...] + yb_ref[...]
    mean = _head_sums(ysum, ones) * (1.0 / RW_HEAD)
    dev = ysum - mean
    var = _head_sums(dev * dev, ones) * (1.0 / RW_HEAD)
    yn = dev * lax.rsqrt(var + RW_LN_EPS) * lng_ref[...] + lnb_ref[...]
    y_rw = (yn + bonus_ref[...].astype(F32)) * g_ref[...].astype(F32)
    gates = _sigmoid(gate_ref[...].astype(F32))
    mix = (gates[:, :D_MODEL] * _bdot(hy_ref[...].astype(BF16), hyp_ref[...])
           + gates[:, D_MODEL:] * _bdot(y_rw.astype(BF16), rwp_ref[...]))
    mixed = _bdot(mix.astype(BF16), wo_ref[...])
    x1 = x_ref[...] + m2_ref[...] * (_rms(mixed) * n1_ref[...])
    x1_ref[...] = x1
    h2 = _rms(x1) * n2_ref[...] * sc4_ref[...] + m3_ref[...]
    h2_ref[...] = h2
    logits = _dot_w3(h2, rwh_ref[...], rwl_ref[...])
    lane = lax.broadcasted_iota(I32, logits.shape, 1)
    logits = jnp.where(lane < N_EXPERTS, logits, -1e30)
    ex = jnp.exp(logits - jnp.max(logits, axis=1, keepdims=True))
    aff = ex / jnp.sum(ex, axis=1, keepdims=True)
    aff_ref[...] = aff
    afft_ref[...] = jnp.transpose(aff)[:N_EXPERTS, :]


def _merge(out_hy, y_f, y_b, bonus, g, gate_cols, x, hy_proj, rw_proj, w_out, ln_g, ln_b, m2, n1post, n2pre, sc4, m3,
           router_w):
    tt = 256
    tok = lambda w: pl.BlockSpec((None, tt, w), lambda b, t: (b, t, 0))
    full = lambda a: pl.BlockSpec(a.shape, lambda b, t: (0,) * a.ndim)
    per_b = pl.BlockSpec((None, 1, D_MODEL), lambda b, t: (b, 0, 0))
    vec = lambda a, n: a.reshape(1, n)
    consts = (hy_proj.astype(BF16), rw_proj.astype(BF16), w_out.astype(BF16), _head_ones(),
              vec(ln_g, RW_WIDTH), vec(ln_b, RW_WIDTH))
    router = _split(_pad2(router_w, D_MODEL, 128))
    return pl.pallas_call(
        _merge_body,
        grid=(BATCH, SEQ // tt),
        in_specs=[tok(HY_WIDTH), tok(RW_WIDTH), tok(RW_WIDTH), tok(RW_WIDTH), tok(RW_WIDTH), tok(GATE_COLS),
                  tok(D_MODEL)]
                 + [full(a) for a in consts]
                 + [per_b, full(vec(n1post, D_MODEL)), full(vec(n2pre, D_MODEL)), per_b, per_b,
                    full(router[0]), full(router[1])],
        out_specs=[tok(D_MODEL), tok(D_MODEL), tok(128),
                   pl.BlockSpec((None, N_EXPERTS, tt), lambda b, t: (b, 0, t))],
        out_shape=[jax.ShapeDtypeStruct((BATCH, SEQ, D_MODEL), F32),
                   jax.ShapeDtypeStruct((BATCH, SEQ, D_MODEL), F32),
                   jax.ShapeDtypeStruct((BATCH, SEQ, 128), F32),
                   jax.ShapeDtypeStruct((BATCH, N_EXPERTS, SEQ), F32)],
        compiler_params=_cparams(("arbitrary", "arbitrary"), VMEM_LIMIT),
        name="merge_router",
    )(out_hy, y_f, y_b, bonus, g, gate_cols, x, *consts, m2, vec(n1post, D_MODEL), vec(n2pre, D_MODEL), sc4, m3,
      *router)


TBLK = 128


def _select_body(afft_ref, key_ref, st_ref):
    ne = N_EXPERTS

    def bisect(_, carry):
        lo, hi = carry
        mid = lo + ((hi - lo) >> 1)
        n_ge = jnp.sum(jnp.where(afft_ref[...] >= pltpu.bitcast(mid, F32), 1.0, 0.0), axis=1, keepdims=True)
        ok = n_ge >= float(CAP)
        return jnp.where(ok, mid, lo), jnp.where(ok, hi, mid)

    lo0 = jnp.zeros((ne, 1), I32)
    hi0 = jnp.full((ne, 1), 0x7F800000, I32)
    thr_bits, _ = lax.fori_loop(0, 32, bisect, (lo0, hi0))
    thr = pltpu.bitcast(thr_bits, F32)
    need = float(CAP) - jnp.sum(jnp.where(afft_ref[...] > thr, 1.0, 0.0), axis=1, keepdims=True)

    ri = lax.broadcasted_iota(I32, (128, 128), 0)
    ci = lax.broadcasted_iota(I32, (128, 128), 1)
    upper = jnp.where(ri <= ci, 1.0, 0.0).astype(BF16)
    off_eq = jnp.zeros((ne, 1), F32)
    off_sel = jnp.zeros((ne, 1), F32)
    lane = lax.broadcasted_iota(I32, (ne, 128), 1)
    starts = jnp.zeros((ne, 128), F32)
    for blk in range(SEQ // TBLK):
        sl = slice(TBLK * blk, TBLK * (blk + 1))
        a_b = afft_ref[:, sl]
        eq_b = jnp.where(a_b == thr, 1.0, 0.0)
        inc = _bdot(eq_b.astype(BF16), upper) + off_eq
        off_eq = inc[:, 127:128]
        sel_b = jnp.where(a_b > thr, 1.0, 0.0) + eq_b * jnp.where(inc - eq_b < need, 1.0, 0.0)
        starts = jnp.where(lane == blk, off_sel, starts)
        cnt = _bdot(sel_b.astype(BF16), upper) + off_sel
        off_sel = cnt[:, 127:128]
        key_ref[:, sl] = cnt * sel_b
    st_ref[...] = starts.astype(I32)


def _compact_body(st_ref, key_ref, aff_ref, out_ref):
    ne = N_EXPERTS
    win = 2 * TBLK
    out_ref[...] = jnp.zeros_like(out_ref)
    lane = lax.broadcasted_iota(I32, (TBLK, 128), 1)
    pos = lax.broadcasted_iota(I32, (TBLK, 128), 0).astype(F32)
    slot_off = lax.broadcasted_iota(I32, (win, TBLK), 0).astype(F32) + 1.0
    res_expert = lax.broadcasted_iota(I32, (win, 128), 1) & (ne - 1)

    def per_block(blk, carry):
        o = pl.multiple_of(blk * TBLK, TBLK)
        a0 = aff_ref[pl.ds(o, TBLK), :]
        a_hi = a0.astype(BF16).astype(F32)
        a1 = a0 - a_hi
        a_mid = a1.astype(BF16).astype(F32)
        a_lo = a1 - a_mid
        blk_f = (blk + jnp.zeros((TBLK, 128), I32)).astype(F32)
        payload = jnp.where(lane < ne, pos, jnp.where(lane < 2 * ne, blk_f, jnp.where(
            lane < 3 * ne, pltpu.roll(a_hi, 2 * ne, 1), jnp.where(
                lane < 4 * ne, pltpu.roll(a_mid, 3 * ne, 1),
                jnp.where(lane < 5 * ne, pltpu.roll(a_lo, 4 * ne, 1), 0.0))))).astype(BF16)
        for ex in range(ne):
            c0 = st_ref[ex, blk]
            base = pl.multiple_of(jnp.minimum(c0 & (-TBLK), CAP - win), TBLK)
            slot1 = slot_off + base.astype(F32)
            onehot = jnp.where(key_ref[ex:ex + 1, pl.ds(o, TBLK)] == slot1, 1.0, 0.0).astype(BF16)
            moved = _bdot(onehot, payload)
            rows = pl.ds(base, win)
            out_ref[rows, :] = out_ref[rows, :] + jnp.where(res_expert == ex, moved, 0.0)
        return carry

    lax.fori_loop(0, SEQ // TBLK, per_block, 0)


def _expert_choice(aff_t, aff):
    key, starts = pl.pallas_call(
        _select_body,
        grid=(BATCH,),
        in_specs=[pl.BlockSpec((None, N_EXPERTS, SEQ), lambda b: (b, 0, 0))],
        out_specs=[pl.BlockSpec((None, N_EXPERTS, SEQ), lambda b: (b, 0, 0)),
                   pl.BlockSpec((None, N_EXPERTS, 128), lambda b: (b, 0, 0))],
        out_shape=[jax.ShapeDtypeStruct((BATCH, N_EXPERTS, SEQ), F32),
                   jax.ShapeDtypeStruct((BATCH, N_EXPERTS, 128), I32)],
        compiler_params=_cparams(("arbitrary",)),
        name="expert_select",
    )(aff_t)
    return pl.pallas_call(
        _compact_body,
        grid=(BATCH,),
        in_specs=[pl.BlockSpec((None, N_EXPERTS, 128), lambda b: (b, 0, 0), memory_space=pltpu.SMEM),
                  pl.BlockSpec((None, N_EXPERTS, SEQ), lambda b: (b, 0, 0)),
                  pl.BlockSpec((None, SEQ, 128), lambda b: (b, 0, 0))],
        out_specs=pl.BlockSpec((None, CAP, 128), lambda b: (b, 0, 0)),
        out_shape=jax.ShapeDtypeStruct((BATCH, CAP, 128), F32),
        compiler_params=_cparams(("arbitrary",), VMEM_LIMIT),
        name="expert_compact",
    )(starts, key, aff)


HALF = D_MODEL // 2


def _gather_body(idx_ref, h_ref, o_ref, buf):
    sub = lax.broadcasted_iota(I32, (8, HALF), 0)

    def group(gi, carry):
        base = pl.multiple_of(gi * 8, 8)
        acc = jnp.zeros((8, HALF), F32)
        for j in range(8):
            row = idx_ref[0, base + j]
            tile = h_ref[pl.ds(pl.multiple_of((row >> 3) << 3, 8), 8), :]
            acc = jnp.where(sub == j, pltpu.roll(tile, (j - row) & 7, 0), acc)
        buf[pl.ds(base, 8), :] = acc
        return carry

    lax.fori_loop(0, CAP // 8, group, 0)
    o_ref[...] = buf[...].astype(BF16)


def _gather(idx3, h2):
    return pl.pallas_call(
        _gather_body,
        grid=(BATCH, 2, N_EXPERTS),
        in_specs=[pl.BlockSpec((None, 1, CAP), lambda b, dh, e: (b * N_EXPERTS + e, 0, 0),
                               memory_space=pltpu.SMEM),
                  pl.BlockSpec((None, SEQ, HALF), lambda b, dh, e: (b, 0, dh))],
        out_specs=pl.BlockSpec((None, None, CAP, HALF), lambda b, dh, e: (b, e, 0, dh)),
        out_shape=jax.ShapeDtypeStruct((BATCH, N_EXPERTS, CAP, D_MODEL), BF16),
        scratch_shapes=[pltpu.VMEM((CAP, HALF), F32)],
        compiler_params=_cparams(("arbitrary", "arbitrary", "arbitrary"), VMEM_LIMIT),
        name="moe_gather",
    )(idx3, h2)


def _ffn_body(x_ref, w1_ref, w3_ref, w2_ref, gt_ref, o_ref):
    f = pl.program_id(2)
    x = x_ref[...]
    a = jnp.dot(x, w1_ref[...].astype(BF16), preferred_element_type=F32)
    b = jnp.dot(x, w3_ref[...].astype(BF16), preferred_element_type=F32)
    hid = (a * _sigmoid(a) * b).astype(BF16)
    part = jnp.dot(hid, w2_ref[...].astype(BF16), preferred_element_type=F32)

    @pl.when(f == 0)
    def _():
        o_ref[...] = part

    @pl.when(f > 0)
    def _():
        o_ref[...] += part

    @pl.when(f == pl.num_programs(2) - 1)
    def _():
        o_ref[...] = o_ref[...] * gt_ref[...]


def _experts(xe, w1, w3, w2, gates4):
    tf = 512
    return pl.pallas_call(
        _ffn_body,
        grid=(BATCH, N_EXPERTS, D_EXPERT // tf),
        in_specs=[pl.BlockSpec((None, None, CAP, D_MODEL), lambda b, e, f: (b, e, 0, 0)),
                  pl.BlockSpec((None, D_MODEL, tf), lambda b, e, f: (e, 0, f)),
                  pl.BlockSpec((None, D_MODEL, tf), lambda b, e, f: (e, 0, f)),
                  pl.BlockSpec((None, tf, D_MODEL), lambda b, e, f: (e, f, 0)),
                  pl.BlockSpec((None, None, CAP, 1), lambda b, e, f: (b, e, 0, 0))],
        out_specs=pl.BlockSpec((None, None, CAP, D_MODEL), lambda b, e, f: (b, e, 0, 0)),
        out_shape=jax.ShapeDtypeStruct((BATCH, N_EXPERTS, CAP, D_MODEL), F32),
        compiler_params=_cparams(("arbitrary", "arbitrary", "arbitrary"), VMEM_LIMIT),
        name="moe_experts",
    )(xe, w1, w3, w2, gates4)


def _scatter_body(idx_ref, ye_ref, o_ref):
    @pl.when(pl.program_id(2) == 0)
    def _():
        o_ref[...] = jnp.zeros_like(o_ref)

    sub = lax.broadcasted_iota(I32, (8, HALF), 0)

    def group(gi, carry):
        base = pl.multiple_of(gi * 8, 8)
        rows8 = ye_ref[pl.ds(base, 8), :]
        for j in range(8):
            row = idx_ref[0, base + j]
            tile = pl.ds(pl.multiple_of((row >> 3) << 3, 8), 8)
            add = jnp.where(sub == (row & 7), jnp.broadcast_to(rows8[j:j + 1, :], (8, HALF)), 0.0)
            o_ref[tile, :] = o_ref[tile, :] + add
        return carry

    lax.fori_loop(0, CAP // 8, group, 0)


def _scatter(idx3, ye):
    return pl.pallas_call(
        _scatter_body,
        grid=(BATCH, 2, N_EXPERTS),
        in_specs=[pl.BlockSpec((None, 1, CAP), lambda b, dh, e: (b * N_EXPERTS + e, 0, 0),
                               memory_space=pltpu.SMEM),
                  pl.BlockSpec((None, None, CAP, HALF), lambda b, dh, e: (b, e, 0, dh))],
        out_specs=pl.BlockSpec((None, SEQ, HALF), lambda b, dh, e: (b, 0, dh)),
        out_shape=jax.ShapeDtypeStruct((BATCH, SEQ, D_MODEL), F32),
        compiler_params=_cparams(("arbitrary", "arbitrary", "arbitrary"), VMEM_LIMIT),
        name="moe_scatter",
    )(idx3, ye)


def _final_body(x1_ref, moe_ref, m5_ref, n_ref, o_ref):
    o_ref[...] = x1_ref[...] + m5_ref[...] * (_rms(moe_ref[...]) * n_ref[...])


def _final(x1, moe, m5, n2post):
    tt = 512
    tok = pl.BlockSpec((None, tt, D_MODEL), lambda b, t: (b, t, 0))
    return pl.pallas_call(
        _final_body,
        grid=(BATCH, SEQ // tt),
        in_specs=[tok, tok, pl.BlockSpec((None, 1, D_MODEL), lambda b, t: (b, 0, 0)),
                  pl.BlockSpec((1, D_MODEL), lambda b, t: (0, 0))],
        out_specs=tok,
        out_shape=jax.ShapeDtypeStruct((BATCH, SEQ, D_MODEL), F32),
        compiler_params=_cparams(("arbitrary", "arbitrary")),
        name="moe_residual",
    )(x1, moe, m5, n2post.reshape(1, D_MODEL))


def _layer(x, c, ctx, c_ctx, p):
    c8 = jnp.zeros((8, D_MODEL), F32).at[:BATCH].set(c).at[BATCH].set(c_ctx)
    mod = _modulation(c8, p['mod_w'], p['mod_b'])
    m_lat = [mod[:BATCH, i * D_MODEL:(i + 1) * D_MODEL].reshape(BATCH, 1, D_MODEL) for i in range(6)]
    m_ctx = [jnp.broadcast_to(mod[BATCH, i * D_MODEL:(i + 1) * D_MODEL], (BATCH, 1, D_MODEL)) for i in range(2)]

    w_in = p['w_in'].astype(BF16)
    w_hy = w_in[:, :HY_COLS]
    w_rw = w_in[:, HY_COLS:HY_COLS + RW_COLS]
    w_gate = w_in[:, HY_COLS + RW_COLS:]
    rw_args = (p['rw_mu'], p['rw_w_up'], p['rw_a_up'], p['rw_w0'], p['rw_a0'], p['rw_g_up'],
               p['rw_k_k'], p['rw_k_a'], p['rw_r_k'])

    (ctx_cols,) = _in_projection(ctx, p['norm1_pre'], 1.0 + m_ctx[1], m_ctx[0], [w_rw], CTX_LEN, 1)
    cr, cv, ckk, ce, ckm, cb, _, _ = _rwkv_prep(ctx_cols, *rw_args)
    zero_state = jnp.zeros((2, BATCH, NQ, RW_HEAD, QW), F32)
    _, _, ctx_states = _rwkv_scan(cr, cv, ckk, ce, ckm, cb, zero_state)

    hy_cols, rw_cols, gate_cols = _in_projection(x, p['norm1_pre'], 1.0 + m_lat[1], m_lat[0],
                                                 [w_hy, w_rw, w_gate], 1024, 2, [F32, F32, BF16])
    out_hy = _hyena_branch(hy_cols, p['hy_conv_w'], p['hy_conv_b'], p['hy_ffn_w1'], p['hy_ffn_b1'],
                           p['hy_ffn_w2'], p['hy_ffn_b2'], p['hy_ffn_w3'], p['hy_freq'], p['hy_bias'])
    r, v, kk, e, km, bvec, bonus, g = _rwkv_prep(rw_cols, *rw_args)
    y_f, y_b, _ = _rwkv_scan(r, v, kk, e, km, bvec, ctx_states)

    x1, h2, aff, aff_t = _merge(out_hy, y_f, y_b, bonus, g, gate_cols, x, p['hy_proj'], p['rw_proj'], p['w_out'],
                                p['rw_ln_g'], p['rw_ln_b'], m_lat[2], p['norm1_post'], p['norm2_pre'],
                                1.0 + m_lat[4], m_lat[3], p['router_w'])
    picked = _expert_choice(aff_t, aff)
    ne = N_EXPERTS
    idx = (TBLK * picked[:, :, ne:2 * ne] + picked[:, :, :ne]).astype(I32)
    gates = (picked[:, :, 2 * ne:3 * ne] + picked[:, :, 3 * ne:4 * ne]) + picked[:, :, 4 * ne:5 * ne]
    idx3 = jnp.swapaxes(idx, 1, 2).reshape(BATCH * ne, 1, CAP)
    gates4 = jnp.swapaxes(gates, 1, 2).reshape(BATCH, ne, CAP, 1)
    xe = _gather(idx3, h2)
    ye = _experts(xe, p['exp_w1'], p['exp_w3'], p['exp_w2'], gates4)
    moe = _scatter(idx3, ye)
    return _final(x1, moe, m_lat[5], p['norm2_post'])


def kernel(x, c, ctx, c_ctx, mod_w, mod_b, norm1_pre, norm1_post, norm2_pre, norm2_post, w_in, hy_conv_w, hy_conv_b, hy_ffn_w1, hy_ffn_b1, hy_ffn_w2, hy_ffn_b2, hy_ffn_w3, hy_freq, hy_bias, hy_proj, rw_mu, rw_w0, rw_w_up, rw_a0, rw_a_up, rw_g_up, rw_k_k, rw_k_a, rw_r_k, rw_ln_g, rw_ln_b, rw_proj, w_out, router_w, exp_w1, exp_w3, exp_w2):
    names = ('mod_w', 'mod_b', 'norm1_pre', 'norm1_post', 'norm2_pre', 'norm2_post', 'w_in', 'hy_conv_w',
             'hy_conv_b', 'hy_ffn_w1', 'hy_ffn_b1', 'hy_ffn_w2', 'hy_ffn_b2', 'hy_ffn_w3', 'hy_freq', 'hy_bias',
             'hy_proj', 'rw_mu', 'rw_w0', 'rw_w_up', 'rw_a0', 'rw_a_up', 'rw_g_up', 'rw_k_k', 'rw_k_a', 'rw_r_k',
             'rw_ln_g', 'rw_ln_b', 'rw_proj', 'w_out', 'router_w', 'exp_w1', 'exp_w3', 'exp_w2')
    vals = (mod_w, mod_b, norm1_pre, norm1_post, norm2_pre, norm2_post, w_in, hy_conv_w, hy_conv_b, hy_ffn_w1,
            hy_ffn_b1, hy_ffn_w2, hy_ffn_b2, hy_ffn_w3, hy_freq, hy_bias, hy_proj, rw_mu, rw_w0, rw_w_up, rw_a0,
            rw_a_up, rw_g_up, rw_k_k, rw_k_a, rw_r_k, rw_ln_g, rw_ln_b, rw_proj, w_out, router_w, exp_w1, exp_w3,
            exp_w2)
    depth = mod_w.shape[0]
    assert depth == 1, "single-layer block: the context stream only provides scan start states"
    p = {n: a[0] for n, a in zip(names, vals)}
    return _layer(x, c, ctx, c_ctx, p)
```

```python
import functools
import math

import numpy as np
import jax
import jax.numpy as jnp
from jax import lax
from jax.experimental import pallas as pl
from jax.experimental.pallas import tpu as pltpu

F32 = jnp.float32
BF16 = jnp.bfloat16
I32 = jnp.int32
HI = lax.Precision.HIGHEST

D_MODEL = 1024
BATCH = 4
SEQ = 8192
CTX_LEN = 256

HY_WIDTH = 512
HY_BANDS = 16
HY_FILTER_ORDER = 64
HY_DECAY_TARGET = 1e-2
HY_DECAY_SHORT = 0.3
HY_DECAY_LONG = 1.5

RW_HEADS = 8
RW_HEAD = 64
RW_WIDTH = RW_HEADS * RW_HEAD
LORA_W = 64
LORA_A = 64
LORA_G = 128
RW_LN_EPS = 64e-5

HY_COLS = 3 * HY_WIDTH
RW_COLS = 3 * RW_WIDTH + LORA_W + LORA_A + LORA_G
GATE_COLS = 2 * D_MODEL

N_EXPERTS = 16
D_EXPERT = 2048
EC_CAPACITY = 2
NORM_EPS = 1e-6
CAP = EC_CAPACITY * SEQ // N_EXPERTS

NFFT = 2 * SEQ
FFT_R = 128
FFT_H = FFT_R // 2
FFT_J = 8
FFT_C = 512
CHUNK = 64
QUAD = 4
QW = QUAD * RW_HEAD
NQ = RW_HEADS // QUAD

VMEM_LIMIT = 56 * 1024 * 1024


def _cparams(sem, vmem=None, flags=None):
    return pltpu.CompilerParams(dimension_semantics=sem, vmem_limit_bytes=vmem, flags=flags)


def _dot_hi(a, b):
    return jnp.dot(a, b, precision=HI, preferred_element_type=F32)


def _bdot(a, b):
    return jnp.dot(a, b, preferred_element_type=F32)


def _split(a):
    hi = a.astype(BF16)
    return hi, (a - hi.astype(F32)).astype(BF16)


def _dot_c(m, x):
    return _bdot(m, x.astype(BF16))


def _dot_w3(x, w_hi, w_lo):
    xh, xl = _split(x)
    return _bdot(xh, w_hi) + (_bdot(xl, w_hi) + _bdot(xh, w_lo))


def _head_sums(x, ones_bf):
    xh, xl = _split(x)
    return _bdot(xh, ones_bf) + _bdot(xl, ones_bf)


def _rms(x):
    return x * lax.rsqrt(jnp.mean(x * x, axis=-1, keepdims=True) + NORM_EPS)


def _sigmoid(x):
    return 1.0 / (1.0 + jnp.exp(-x))


def _pad2(a, rows, cols):
    return jnp.zeros((rows, cols), F32).at[:a.shape[0], :a.shape[1]].set(a)


def _mod_body(c_ref, w_ref, b_ref, o_ref):
    c = c_ref[...]
    o_ref[...] = _dot_hi(c * _sigmoid(c), w_ref[...]) + b_ref[...]


def _modulation(c8, mod_w, mod_b):
    n = mod_w.shape[1]
    tn = 1536
    return pl.pallas_call(
        _mod_body,
        grid=(n // tn,),
        in_specs=[pl.BlockSpec((8, D_MODEL), lambda j: (0, 0)),
                  pl.BlockSpec((D_MODEL, tn), lambda j: (0, j)),
                  pl.BlockSpec((1, tn), lambda j: (0, j))],
        out_specs=pl.BlockSpec((8, tn), lambda j: (0, j)),
        out_shape=jax.ShapeDtypeStruct((8, n), F32),
        compiler_params=_cparams(("arbitrary",)),
        name="modulation",
    )(c8, mod_w, mod_b.reshape(1, n))


def _inproj_body(nw, x_ref, g_ref, sc_ref, sh_ref, *refs):
    w_refs, o_refs, h_ref = refs[:nw], refs[nw:2 * nw], refs[2 * nw]

    @pl.when(pl.program_id(2) == 0)
    def _():
        h = _rms(x_ref[...]) * g_ref[...] * sc_ref[...] + sh_ref[...]
        h_ref[...] = h.astype(BF16)

    h = h_ref[...]
    for w_ref, o_ref in zip(w_refs, o_refs):
        o_ref[...] = jnp.dot(h, w_ref[...], preferred_element_type=F32).astype(o_ref.dtype)


def _in_projection(x, g, scale1p, shift, weights, tm, nsplit, out_dtypes=None):
    out_dtypes = out_dtypes or [F32] * len(weights)
    bx, tx, _ = x.shape
    nw = len(weights)
    tns = [w.shape[1] // nsplit for w in weights]
    in_specs = [pl.BlockSpec((None, tm, D_MODEL), lambda b, m, n: (b, m, 0)),
                pl.BlockSpec((1, D_MODEL), lambda b, m, n: (0, 0)),
                pl.BlockSpec((None, 1, D_MODEL), lambda b, m, n: (b, 0, 0)),
                pl.BlockSpec((None, 1, D_MODEL), lambda b, m, n: (b, 0, 0))]
    in_specs += [pl.BlockSpec((D_MODEL, tn), lambda b, m, n: (0, n)) for tn in tns]
    out_specs = [pl.BlockSpec((None, tm, tn), lambda b, m, n: (b, m, n)) for tn in tns]
    out_shape = [jax.ShapeDtypeStruct((bx, tx, w.shape[1]), dt) for w, dt in zip(weights, out_dtypes)]
    return pl.pallas_call(
        functools.partial(_inproj_body, nw),
        grid=(bx, tx // tm, nsplit),
        in_specs=in_specs,
        out_specs=out_specs,
        out_shape=out_shape,
        scratch_shapes=[pltpu.VMEM((tm, D_MODEL), BF16)],
        compiler_params=_cparams(("arbitrary", "arbitrary", "arbitrary"), VMEM_LIMIT),
        name="in_projection",
    )(x, g.reshape(1, D_MODEL), scale1p, shift, *weights)


def _neighbours(cur, pv_ref, nx_ref):
    tt = cur.shape[0]
    t = pl.program_id(1)
    first = (t > 0).astype(F32)
    last = (t < pl.num_programs(1) - 1).astype(F32)
    prev_row = pv_ref[7:8, :] * first
    next_row = nx_ref[0:1, :] * last
    rows = lax.broadcasted_iota(I32, cur.shape, 0)
    prev = jnp.where(rows == 0, prev_row, pltpu.roll(cur, 1, 0))
    nxt = jnp.where(rows == tt - 1, next_row, pltpu.roll(cur, tt - 1, 0))
    return prev, nxt


def _halo_specs(tt, t_total, width):
    r8 = tt // 8
    last8 = t_total // 8 - 1
    return [pl.BlockSpec((None, tt, width), lambda b, t: (b, t, 0)),
            pl.BlockSpec((None, 8, width), lambda b, t: (b, jnp.maximum(t * r8 - 1, 0), 0)),
            pl.BlockSpec((None, 8, width), lambda b, t: (b, jnp.minimum((t + 1) * r8, last8), 0))]


def _hy_pre_body(cur_ref, pv_ref, nx_ref, w_ref, b_ref, z_ref, x0_ref):
    cur = cur_ref[...]
    prev, nxt = _neighbours(cur, pv_ref, nx_ref)
    u = w_ref[0:1, :] * prev + w_ref[1:2, :] * cur + w_ref[2:3, :] * nxt + b_ref[...]
    z_ref[...] = u[:, 2 * HY_WIDTH:] * u[:, HY_WIDTH:2 * HY_WIDTH]
    x0_ref[...] = u[:, :HY_WIDTH]


def _hyena_pre(hy_cols, conv_w, conv_b):
    tt = 512
    spec_o = pl.BlockSpec((None, tt, HY_WIDTH), lambda b, t: (b, t, 0))
    return pl.pallas_call(
        _hy_pre_body,
        grid=(BATCH, SEQ // tt),
        in_specs=_halo_specs(tt, SEQ, HY_COLS) + [
            pl.BlockSpec((3, HY_COLS), lambda b, t: (0, 0)),
            pl.BlockSpec((1, HY_COLS), lambda b, t: (0, 0))],
        out_specs=[spec_o, spec_o],
        out_shape=[jax.ShapeDtypeStruct((BATCH, SEQ, HY_WIDTH), F32)] * 2,
        compiler_params=_cparams(("arbitrary", "arbitrary"), VMEM_LIMIT),
        name="hyena_pre",
    )(hy_cols, hy_cols, hy_cols, conv_w, conv_b.reshape(1, HY_COLS))


def _filt_body(bands_ref, w1t_ref, w1c_ref, w1s_ref, b1_ref, w2_ref, b2_ref, w3_ref, fr_ref, dl_ref,
               f_ref, s_ref):
    i = pl.program_id(0)
    tt = f_ref.shape[0]
    rowi = lax.broadcasted_iota(I32, (tt, 1), 0) + i * tt
    rowf = rowi.astype(F32)
    tpos = rowf * (1.0 / (SEQ - 1))
    arg = (rowf * (2.0 * math.pi / SEQ)) * bands_ref[...]
    pre = tpos * w1t_ref[...] + _dot_hi(jnp.cos(arg), w1c_ref[...]) - _dot_hi(jnp.sin(arg), w1s_ref[...]) + b1_ref[...]
    freq = fr_ref[...]
    hid = jnp.sin(freq * pre)
    hid = jnp.sin(freq * (_dot_hi(hid, w2_ref[...]) + b2_ref[...]))
    filt = _dot_hi(hid, w3_ref[...]) * jnp.exp(-tpos * dl_ref[...])
    lane = lax.broadcasted_iota(I32, filt.shape, 1)
    rows = lax.broadcasted_iota(I32, filt.shape, 0) + i * tt
    filt = jnp.where((rows == 0) & (lane >= HY_WIDTH), 0.0, filt)
    f_ref[...] = filt

    @pl.when(i == 0)
    def _():
        s_ref[...] = jnp.zeros_like(s_ref)

    s_ref[...] += jnp.sum(jnp.abs(filt), axis=0, keepdims=True)


def _hyena_filter(w1, b1, w2, b2, w3, freq):
    tt = 512
    bands = np.zeros((1, 128), np.float32)
    bands[0, :HY_BANDS] = np.linspace(1e-4, HY_BANDS - 1, HY_BANDS, dtype=np.float32)
    deltas = np.abs(np.linspace(math.log(HY_DECAY_TARGET) / HY_DECAY_LONG,
                                math.log(HY_DECAY_TARGET) / HY_DECAY_SHORT, HY_WIDTH, dtype=np.float32))
    dl = np.concatenate([deltas, deltas])[None, :].astype(np.float32)
    args = (jnp.asarray(bands),
            _pad2(w1[0:1], 1, 128),
            _pad2(w1[1:1 + HY_BANDS], 128, 128),
            _pad2(w1[1 + HY_BANDS:], 128, 128),
            _pad2(b1[None], 1, 128),
            _pad2(w2, 128, 128),
            _pad2(b2[None], 1, 128),
            _pad2(w3, 128, 2 * HY_WIDTH),
            _pad2(freq[None], 1, 128),
            jnp.asarray(dl))
    full = lambda a: pl.BlockSpec(a.shape, lambda i: (0, 0))
    return pl.pallas_call(
        _filt_body,
        grid=(SEQ // tt,),
        in_specs=[full(a) for a in args],
        out_specs=[pl.BlockSpec((tt, 2 * HY_WIDTH), lambda i: (i, 0)),
                   pl.BlockSpec((1, 2 * HY_WIDTH), lambda i: (0, 0))],
        out_shape=[jax.ShapeDtypeStruct((SEQ, 2 * HY_WIDTH), F32),
                   jax.ShapeDtypeStruct((1, 2 * HY_WIDTH), F32)],
        compiler_params=_cparams(("arbitrary",)),
        name="hyena_filter",
    )(*args)


def _fft_tables():
    k = np.arange(FFT_R)
    n = np.arange(FFT_H)
    th = 2.0 * np.pi * np.outer(k, n) / FFT_R
    f1 = np.stack([np.cos(th), -np.sin(th)])
    eye = np.eye(FFT_J)
    g1 = np.einsum('pkn,jm->pkjnm', f1, eye).reshape(2 * FFT_R * FFT_J, FFT_H * FFT_J)
    f3 = np.stack([np.cos(th.T), -np.sin(th.T)]) / NFFT
    g3 = np.einsum('pnk,jm->njpkm', f3, eye).reshape(FFT_H * FFT_J, 2 * FFT_R * FFT_J)
    th2 = 2.0 * np.pi * np.outer(k, k) / FFT_R
    c, s = np.cos(th2), np.sin(th2)
    fc = np.block([[c, s], [-s, c]])
    fi = np.block([[c, -s], [s, c]])
    return tuple(jnp.asarray(a, F32) for a in (g1, g3, fc, fi))


def _twiddle_tables():
    k = np.arange(FFT_R)
    ang = 2.0 * np.pi * (np.outer(k, k) % NFFT) / NFFT
    first = ang.reshape(FFT_R, FFT_R // FFT_J, FFT_J).transpose(1, 0, 2).reshape(FFT_R // FFT_J, FFT_R * FFT_J, 1)
    mid = ang.reshape(FFT_R // MID_K, MID_K, FFT_R).transpose(0, 2, 1)
    return tuple(jnp.asarray(f(a), F32) for a in (first, mid) for f in (np.cos, np.sin))


def _fft1_body(z_ref, g_ref, cs_ref, sn_ref, o_ref):
    c = z_ref.shape[-1]
    z = z_ref[...].reshape(FFT_H * FFT_J, c)
    a = _dot_c(g_ref[...], z)
    half = FFT_R * FFT_J
    ar, ai = a[:half], a[half:]
    cs, sn = cs_ref[...], sn_ref[...]
    o_ref[0] = (ar * cs + ai * sn).reshape(FFT_R, FFT_J, c)
    o_ref[1] = (ai * cs - ar * sn).reshape(FFT_R, FFT_J, c)


def _fft_stage1(z4, g1, tw):
    bz, _, _, c = z4.shape
    tab = pl.BlockSpec(g1.shape, lambda b, n, cb: (0, 0))
    col = pl.BlockSpec((None, FFT_R * FFT_J, 1), lambda b, n, cb: (n, 0, 0))
    return pl.pallas_call(
        _fft1_body,
        grid=(bz, FFT_R // FFT_J, c // FFT_C),
        in_specs=[pl.BlockSpec((None, FFT_H, FFT_J, FFT_C), lambda b, n, cb: (b, 0, n, cb)), tab, col, col],
        out_specs=pl.BlockSpec((None, 2, FFT_R, FFT_J, FFT_C), lambda b, n, cb: (b, 0, 0, n, cb)),
        out_shape=jax.ShapeDtypeStruct((bz, 2, FFT_R, FFT_R, c), F32),
        compiler_params=_cparams(("arbitrary", "arbitrary", "arbitrary"), VMEM_LIMIT),
        name="fft_stage1",
    )(z4, g1, *tw)


def _filt_spec_body(a_ref, f_ref, h_ref):
    x = _dot_c(f_ref[...], jnp.concatenate([a_ref[0], a_ref[1]], axis=0))
    xr, xi = x[:FFT_R], x[FFT_R:]
    h_ref[0] = xr[:, :HY_WIDTH] + xr[:, HY_WIDTH:]
    h_ref[1] = xi[:, :HY_WIDTH] - xi[:, HY_WIDTH:]


def _filter_spectrum(a, fc):
    tab = pl.BlockSpec(fc.shape, lambda k: (0, 0))
    return pl.pallas_call(
        _filt_spec_body,
        grid=(FFT_R,),
        in_specs=[pl.BlockSpec((None, 2, None, FFT_R, 2 * HY_WIDTH), lambda k: (0, 0, k, 0, 0)), tab],
        out_specs=pl.BlockSpec((None, 2, FFT_R, HY_WIDTH), lambda k: (k, 0, 0, 0)),
        out_shape=jax.ShapeDtypeStruct((FFT_R, 2, FFT_R, HY_WIDTH), F32),
        compiler_params=_cparams(("arbitrary",)),
        name="filter_spectrum",
    )(a, fc)


MID_K = 4


def _fft_mid_body(a_ref, h_ref, fc_ref, fi_ref, cs_ref, sn_ref, o_ref):
    for i in range(MID_K):
        x = _dot_c(fc_ref[...], jnp.concatenate([a_ref[0, i], a_ref[1, i]], axis=0))
        xr, xi = x[:FFT_R], x[FFT_R:]
        hr, him = h_ref[i, 0], h_ref[i, 1]
        yr = xr * hr - xi * him
        yi = xr * him + xi * hr
        p = _dot_c(fi_ref[...], jnp.concatenate([yr, yi], axis=0))
        pr, pim = p[:FFT_R], p[FFT_R:]
        cs, sn = cs_ref[:, i:i + 1], sn_ref[:, i:i + 1]
        o_ref[0, i] = pr * cs - pim * sn
        o_ref[1, i] = pr * sn + pim * cs


def _fft_mid(a, h, fc, fi, tw):
    bz = a.shape[0]
    blk = pl.BlockSpec((None, 2, MID_K, FFT_R, HY_WIDTH), lambda k, b: (b, 0, k, 0, 0))
    tab = pl.BlockSpec(fc.shape, lambda k, b: (0, 0))
    col = pl.BlockSpec((None, FFT_R, MID_K), lambda k, b: (k, 0, 0))
    return pl.pallas_call(
        _fft_mid_body,
        grid=(FFT_R // MID_K, bz),
        in_specs=[blk, pl.BlockSpec((MID_K, 2, FFT_R, HY_WIDTH), lambda k, b: (k, 0, 0, 0)), tab, tab, col, col],
        out_specs=blk,
        out_shape=jax.ShapeDtypeStruct(a.shape, F32),
        compiler_params=_cparams(("arbitrary", "arbitrary"), VMEM_LIMIT),
        name="fft_mid",
    )(a, h, fc, fi, *tw)


def _ifft_body(q_ref, g_ref, z_ref, x0_ref, inv_ref, hb_ref, o_ref):
    c = HY_WIDTH
    q = q_ref[...].reshape(2 * FFT_R * FFT_J, c)
    conv = _dot_c(g_ref[...], q) * inv_ref[...]
    z = z_ref[...].reshape(FFT_H * FFT_J, c)
    x0 = x0_ref[...].reshape(FFT_H * FFT_J, c)
    o_ref[...] = ((conv + hb_ref[...] * z) * x0).reshape(FFT_H, FFT_J, c)


def _ifft_gate(q, g3, z4, x04, inv_norm, hy_bias):
    row = pl.BlockSpec((None, FFT_H, FFT_J, HY_WIDTH), lambda b, n: (b, 0, n, 0))
    vec = pl.BlockSpec((1, HY_WIDTH), lambda b, n: (0, 0))
    tab = pl.BlockSpec(g3.shape, lambda b, n: (0, 0))
    return pl.pallas_call(
        _ifft_body,
        grid=(BATCH, FFT_R // FFT_J),
        in_specs=[pl.BlockSpec((None, 2, FFT_R, FFT_J, HY_WIDTH), lambda b, n: (b, 0, 0, n, 0)),
                  tab, row, row, vec, vec],
        out_specs=row,
        out_shape=jax.ShapeDtypeStruct(z4.shape, F32),
        compiler_params=_cparams(("arbitrary", "arbitrary"), VMEM_LIMIT),
        name="ifft_gate",
    )(q, g3, z4, x04, inv_norm, hy_bias)


def _hyena_branch(hy_cols, conv_w, conv_b, w1, b1, w2, b2, w3, freq, hy_bias):
    g1, g3, fc, fi = (t.astype(BF16) for t in _fft_tables())
    z, x0 = _hyena_pre(hy_cols, conv_w, conv_b)
    filt, l1 = _hyena_filter(w1, b1, w2, b2, w3, freq)
    inv_norm = 1.0 / (l1[:, :HY_WIDTH] + l1[:, HY_WIDTH:])
    tw1c, tw1s, tw2c, tw2s = _twiddle_tables()
    fa = _fft_stage1(filt.reshape(1, FFT_H, FFT_R, 2 * HY_WIDTH), g1, (tw1c, tw1s))
    h = _filter_spectrum(fa, fc)
    z4 = z.reshape(BATCH, FFT_H, FFT_R, HY_WIDTH)
    za = _fft_stage1(z4, g1, (tw1c, tw1s))
    q = _fft_mid(za, h, fc, fi, (tw2c, tw2s))
    out = _ifft_gate(q, g3, z4, x0.reshape(z4.shape), inv_norm, hy_bias.reshape(1, HY_WIDTH))
    return out.reshape(BATCH, SEQ, HY_WIDTH)


def _rw_prep_body(cur_ref, pv_ref, nx_ref, mu_ref, w2h_ref, w2l_ref, w0_ref, a0_ref, guph_ref, gupl_ref, kkw_ref,
                  ka_ref, rk_ref, ones_ref, r_o, v_o, kk_o, e_o, km_o, b_o, bonus_o, g_o):
    cur = cur_ref[...]
    prev, nxt = _neighbours(cur, pv_ref, nx_ref)
    z = cur + mu_ref[0:1, :] * (prev - cur) + mu_ref[1:2, :] * (nxt - cur)
    w_ = RW_WIDTH
    r, k, v = z[:, :w_], z[:, w_:2 * w_], z[:, 2 * w_:3 * w_]
    ll = z[:, 3 * w_:3 * w_ + 128]
    lg = z[:, 3 * w_ + 128:]
    lane = lax.broadcasted_iota(I32, ll.shape, 1)
    tl = jnp.where(lane < LORA_W, jnp.tanh(ll), ll)
    ones = ones_ref[...]
    kkr = k * kkw_ref[...]
    kk = kkr * lax.rsqrt(jnp.maximum(_head_sums(kkr * kkr, ones), 1e-24))
    bonus = jnp.zeros_like(r)
    for d in range(2):
        proj = _dot_w3(tl, w2h_ref[d], w2l_ref[d])
        wl = proj[:, :w_] + w0_ref[d:d + 1, :]
        al = proj[:, w_:] + a0_ref[d:d + 1, :]
        nwl = -wl
        w = -(jnp.maximum(nwl, 0.0) + jnp.log(1.0 + jnp.exp(-jnp.abs(nwl)))) - 0.5
        a = _sigmoid(al)
        km = k * (1.0 + (a - 1.0) * ka_ref[...])
        e_o[d] = jnp.exp(w)
        km_o[d] = km
        b_o[d] = kk * a
        bonus = bonus + _head_sums(r * km * rk_ref[...], ones)
    r_o[...] = r
    v_o[...] = v.astype(BF16)
    kk_o[...] = kk
    bonus_o[...] = (bonus * v).astype(BF16)
    g_o[...] = _dot_w3(_sigmoid(lg), guph_ref[...], gupl_ref[...]).astype(BF16)


def _head_ones():
    h = np.arange(RW_WIDTH) // RW_HEAD
    return jnp.asarray((h[:, None] == h[None, :]).astype(np.float32), BF16)


def _rwkv_prep(rw_cols, mu, w_up, a_up, w0, a0, g_up, k_k, k_a, r_k):
    bx, tx, _ = rw_cols.shape
    tt = min(512, tx)
    w2 = jnp.zeros((2, 128, 2 * RW_WIDTH), F32)
    w2 = w2.at[:, :LORA_W, :RW_WIDTH].set(w_up).at[:, LORA_W:, RW_WIDTH:].set(a_up)
    vec = lambda a: a.reshape(1, RW_WIDTH)
    full2 = lambda a: pl.BlockSpec(a.shape, lambda b, t: (0,) * a.ndim)
    args = (mu, *_split(w2), w0, a0, *_split(g_up), vec(k_k), vec(k_a), vec(r_k), _head_ones())
    one = pl.BlockSpec((None, tt, RW_WIDTH), lambda b, t: (b, t, 0))
    two = pl.BlockSpec((2, None, tt, RW_WIDTH), lambda b, t: (0, b, t, 0))
    s1 = jax.ShapeDtypeStruct((bx, tx, RW_WIDTH), F32)
    s1h = jax.ShapeDtypeStruct((bx, tx, RW_WIDTH), BF16)
    s2 = jax.ShapeDtypeStruct((2, bx, tx, RW_WIDTH), F32)
    return pl.pallas_call(
        _rw_prep_body,
        grid=(bx, tx // tt),
        in_specs=_halo_specs(tt, tx, RW_COLS) + [full2(a) for a in args],
        out_specs=[one, one, one, two, two, two, one, one],
        out_shape=[s1, s1h, s1, s2, s2, s2, s1h, s1h],
        compiler_params=_cparams(("arbitrary", "arbitrary"), VMEM_LIMIT),
        name="rwkv_prep",
    )(rw_cols, rw_cols, rw_cols, *args)


def _chunk_cumsum(e, fwd):
    row = lax.broadcasted_iota(I32, e.shape, 0)
    c = e
    s = 1
    while s < CHUNK:
        if fwd:
            c = c + jnp.where(row >= s, pltpu.roll(c, s, 0), 0.0)
        else:
            c = c + jnp.where(row < CHUNK - s, pltpu.roll(c, CHUNK - s, 0), 0.0)
        s *= 2
    return c


def _block_diag(x4, lane_head):
    return jnp.concatenate([jnp.where(lane_head == h, x4, 0.0) for h in range(QUAD)], axis=0).astype(BF16)


def _mm(a, b):
    return jnp.dot(a.astype(BF16), b, preferred_element_type=F32)


def _mm_nt(a, b):
    return lax.dot_general(a.astype(BF16), b, (((1,), (1,)), ((), ())), preferred_element_type=F32)


def _scan_chain(fwd, r, v, kk, e, km, bv, s_ref, y_ref):
    n = CHUNK
    cum = _chunk_cumsum(e, fwd)
    tot = jnp.sum(e, axis=0, keepdims=True)
    grow = jnp.exp(cum)
    at = -kk * jnp.exp(e - cum)
    bt = bv * grow
    kt = km * grow
    rt = r * jnp.exp(-cum)
    tail = jnp.exp(cum - tot)
    bb = bv * tail
    kb = km * tail
    p_all = jnp.exp(-tot)

    row = lax.broadcasted_iota(I32, (n, QW), 0)
    lane = lax.broadcasted_iota(I32, (n, QW), 1)
    col = lane & (RW_HEAD - 1)
    lane_head = lane >> 6
    incl = (row >= col) if fwd else (row <= col)
    strict = (row > col) if fwd else (row < col)
    eye = jnp.where(row == col, 1.0, 0.0)
    bd = lambda x4: _block_diag(x4, lane_head)

    def quad_steps(q):
        ql = slice(QW * q, QW * (q + 1))
        a4, v4 = at[:, ql], v[:, ql]
        ar = jnp.concatenate([a4, rt[:, ql]], axis=0)
        bd_v = bd(v4)
        gb = _mm_nt(ar, bd(bt[:, ql]))
        gk = _mm_nt(ar, bd(kt[:, ql]))
        yield
        m = jnp.where(strict, gb[:n], 0.0)
        g_rb = jnp.where(incl, gb[n:], 0.0)
        g_rk = jnp.where(incl, gk[n:], 0.0)
        w4 = _mm(jnp.where(strict, gk[:n], 0.0), bd_v)
        t4 = eye + m
        p = _mm(m, bd(m))
        s4 = s_ref[q]
        sa = _mm_nt(ar, bd(s4))
        yield
        for level in range(1, 6):
            bd_p = bd(p)
            if level < 5:
                both = _mm(jnp.concatenate([p, t4], axis=0), bd_p)
                p = both[:n]
                t4 = t4 + both[n:]
            else:
                t4 = t4 + _mm(t4, bd_p)
            yield
        u4 = _mm(t4, bd(sa[:n] + w4))
        yield
        y_ref[:, ql] = sa[n:] + _mm(g_rb, bd(u4)) + _mm(g_rk, bd_v)
        uv = jnp.concatenate([u4, v4], axis=0).astype(BF16)
        bk = jnp.concatenate([bb[:, ql], kb[:, ql]], axis=0).astype(BF16)
        full = lax.dot_general(uv, bk, (((0,), (0,)), ((), ())), preferred_element_type=F32)
        upd = s4 * p_all[:, ql]
        for h in range(QUAD):
            upd = upd + jnp.where(lane_head == h, full[RW_HEAD * h:RW_HEAD * (h + 1)], 0.0)
        s_ref[q] = upd

    return [quad_steps(q) for q in range(NQ)]


def _round_robin(gens):
    gens = list(gens)
    while gens:
        for g in list(gens):
            try:
                next(g)
            except StopIteration:
                gens.remove(g)


def _rw_scan_body(nb, rf_ref, vf_ref, kf_ref, rb_ref, vb_ref, kb_ref, ef_ref, mf_ref, bf_ref, eb_ref, mb_ref, bb_ref,
                  s0_ref, yf_ref, yb_ref, sf_ref, s_scr):
    c = pl.program_id(1)

    @pl.when(c == 0)
    def _():
        s_scr[...] = s0_ref[...]

    steps = []
    for i in range(nb):
        steps += _scan_chain(True, rf_ref[i], vf_ref[i].astype(F32), kf_ref[i], ef_ref[i], mf_ref[i], bf_ref[i],
                             s_scr.at[0, i], yf_ref.at[i])
        steps += _scan_chain(False, rb_ref[i], vb_ref[i].astype(F32), kb_ref[i], eb_ref[i], mb_ref[i], bb_ref[i],
                             s_scr.at[1, i], yb_ref.at[i])
    _round_robin(steps)

    @pl.when(c == pl.num_programs(1) - 1)
    def _():
        sf_ref[...] = s_scr[...]


def _rwkv_scan(r, v, kk, e, km, bvec, s0):
    bx, tx, _ = r.shape
    nc = tx // CHUNK
    nb = 4
    fpos = lambda g, c: (g, c, 0)
    bpos = lambda g, c: (g, nc - 1 - c, 0)
    tok = lambda f: pl.BlockSpec((nb, CHUNK, RW_WIDTH), f)
    tokd = lambda d, f: pl.BlockSpec((None, nb, CHUNK, RW_WIDTH), lambda g, c: (d,) + f(g, c))
    st = pl.BlockSpec((2, nb, NQ, RW_HEAD, QW), lambda g, c: (0, g, 0, 0, 0))
    y_shape = jax.ShapeDtypeStruct((bx, tx, RW_WIDTH), F32)
    return pl.pallas_call(
        functools.partial(_rw_scan_body, nb),
        grid=(bx // nb, nc),
        in_specs=[tok(fpos)] * 3 + [tok(bpos)] * 3 + [tokd(0, fpos)] * 3 + [tokd(1, bpos)] * 3 + [st],
        out_specs=[tok(fpos), tok(bpos), st],
        out_shape=[y_shape, y_shape, jax.ShapeDtypeStruct((2, bx, NQ, RW_HEAD, QW), F32)],
        scratch_shapes=[pltpu.VMEM((2, nb, NQ, RW_HEAD, QW), F32)],
        compiler_params=_cparams(("arbitrary", "arbitrary"), VMEM_LIMIT),
        name="rwkv_scan",
    )(r, v, kk, r, v, kk, e, km, bvec, e, km, bvec, s0)


def _merge_body(hy_ref, yf_ref, yb_ref, bonus_ref, g_ref, gate_ref, x_ref, hyp_ref, rwp_ref, wo_ref, ones_ref,
                lng_ref, lnb_ref, m2_ref, n1_ref, n2_ref, sc4_ref, m3_ref, rwh_ref, rwl_ref,
                x1_ref, h2_ref, aff_ref, afft_ref):
    ones = ones_ref[...]
    ysum = yf_ref[...] + yb_ref[...]
    mean = _head_sums(ysum, ones) * (1.0 / RW_HEAD)
    dev = ysum - mean
    var = _head_sums(dev * dev, ones) * (1.0 / RW_HEAD)
    yn = dev * lax.rsqrt(var + RW_LN_EPS) * lng_ref[...] + lnb_ref[...]
    y_rw = (yn + bonus_ref[...].astype(F32)) * g_ref[...].astype(F32)
    gates = _sigmoid(gate_ref[...].astype(F32))
    mix = (gates[:, :D_MODEL] * _bdot(hy_ref[...].astype(BF16), hyp_ref[...])
           + gates[:, D_MODEL:] * _bdot(y_rw.astype(BF16), rwp_ref[...]))
    mixed = _bdot(mix.astype(BF16), wo_ref[...])
    x1 = x_ref[...] + m2_ref[...] * (_rms(mixed) * n1_ref[...])
    x1_ref[...] = x1
    h2 = _rms(x1) * n2_ref[...] * sc4_ref[...] + m3_ref[...]
    h2_ref[...] = h2
    logits = _dot_w3(h2, rwh_ref[...], rwl_ref[...])
    lane = lax.broadcasted_iota(I32, logits.shape, 1)
    logits = jnp.where(lane < N_EXPERTS, logits, -1e30)
    ex = jnp.exp(logits - jnp.max(logits, axis=1, keepdims=True))
    aff = ex / jnp.sum(ex, axis=1, keepdims=True)
    aff_ref[...] = aff
    afft_ref[...] = jnp.transpose(aff)[:N_EXPERTS, :]


def _merge(out_hy, y_f, y_b, bonus, g, gate_cols, x, hy_proj, rw_proj, w_out, ln_g, ln_b, m2, n1post, n2pre, sc4, m3,
           router_w):
    tt = 256
    tok = lambda w: pl.BlockSpec((None, tt, w), lambda b, t: (b, t, 0))
    full = lambda a: pl.BlockSpec(a.shape, lambda b, t: (0,) * a.ndim)
    per_b = pl.BlockSpec((None, 1, D_MODEL), lambda b, t: (b, 0, 0))
    vec = lambda a, n: a.reshape(1, n)
    consts = (hy_proj.astype(BF16), rw_proj.astype(BF16), w_out.astype(BF16), _head_ones(),
              vec(ln_g, RW_WIDTH), vec(ln_b, RW_WIDTH))
    router = _split(_pad2(router_w, D_MODEL, 128))
    return pl.pallas_call(
        _merge_body,
        grid=(BATCH, SEQ // tt),
        in_specs=[tok(HY_WIDTH), tok(RW_WIDTH), tok(RW_WIDTH), tok(RW_WIDTH), tok(RW_WIDTH), tok(GATE_COLS),
                  tok(D_MODEL)]
                 + [full(a) for a in consts]
                 + [per_b, full(vec(n1post, D_MODEL)), full(vec(n2pre, D_MODEL)), per_b, per_b,
                    full(router[0]), full(router[1])],
        out_specs=[tok(D_MODEL), tok(D_MODEL), tok(128),
                   pl.BlockSpec((None, N_EXPERTS, tt), lambda b, t: (b, 0, t))],
        out_shape=[jax.ShapeDtypeStruct((BATCH, SEQ, D_MODEL), F32),
                   jax.ShapeDtypeStruct((BATCH, SEQ, D_MODEL), F32),
                   jax.ShapeDtypeStruct((BATCH, SEQ, 128), F32),
                   jax.ShapeDtypeStruct((BATCH, N_EXPERTS, SEQ), F32)],
        compiler_params=_cparams(("arbitrary", "arbitrary"), VMEM_LIMIT),
        name="merge_router",
    )(out_hy, y_f, y_b, bonus, g, gate_cols, x, *consts, m2, vec(n1post, D_MODEL), vec(n2pre, D_MODEL), sc4, m3,
      *router)


TBLK = 128


def _select_body(afft_ref, key_ref, st_ref):
    ne = N_EXPERTS

    def bisect(_, carry):
        lo, hi = carry
        mid = lo + ((hi - lo) >> 1)
        n_ge = jnp.sum(jnp.where(afft_ref[...] >= pltpu.bitcast(mid, F32), 1.0, 0.0), axis=1, keepdims=True)
        ok = n_ge >= float(CAP)
        return jnp.where(ok, mid, lo), jnp.where(ok, hi, mid)

    lo0 = jnp.zeros((ne, 1), I32)
    hi0 = jnp.full((ne, 1), 0x7F800000, I32)
    thr_bits, _ = lax.fori_loop(0, 32, bisect, (lo0, hi0))
    thr = pltpu.bitcast(thr_bits, F32)
    need = float(CAP) - jnp.sum(jnp.where(afft_ref[...] > thr, 1.0, 0.0), axis=1, keepdims=True)

    ri = lax.broadcasted_iota(I32, (128, 128), 0)
    ci = lax.broadcasted_iota(I32, (128, 128), 1)
    upper = jnp.where(ri <= ci, 1.0, 0.0).astype(BF16)
    off_eq = jnp.zeros((ne, 1), F32)
    off_sel = jnp.zeros((ne, 1), F32)
    lane = lax.broadcasted_iota(I32, (ne, 128), 1)
    starts = jnp.zeros((ne, 128), F32)
    for blk in range(SEQ // TBLK):
        sl = slice(TBLK * blk, TBLK * (blk + 1))
        a_b = afft_ref[:, sl]
        eq_b = jnp.where(a_b == thr, 1.0, 0.0)
        inc = _bdot(eq_b.astype(BF16), upper) + off_eq
        off_eq = inc[:, 127:128]
        sel_b = jnp.where(a_b > thr, 1.0, 0.0) + eq_b * jnp.where(inc - eq_b < need, 1.0, 0.0)
        starts = jnp.where(lane == blk, off_sel, starts)
        cnt = _bdot(sel_b.astype(BF16), upper) + off_sel
        off_sel = cnt[:, 127:128]
        key_ref[:, sl] = cnt * sel_b
    st_ref[...] = starts.astype(I32)


def _compact_body(st_ref, key_ref, aff_ref, out_ref):
    ne = N_EXPERTS
    win = 2 * TBLK
    out_ref[...] = jnp.zeros_like(out_ref)
    lane = lax.broadcasted_iota(I32, (TBLK, 128), 1)
    pos = lax.broadcasted_iota(I32, (TBLK, 128), 0).astype(F32)
    slot_off = lax.broadcasted_iota(I32, (win, TBLK), 0).astype(F32) + 1.0
    res_expert = lax.broadcasted_iota(I32, (win, 128), 1) & (ne - 1)

    def per_block(blk, carry):
        o = pl.multiple_of(blk * TBLK, TBLK)
        a0 = aff_ref[pl.ds(o, TBLK), :]
        a_hi = a0.astype(BF16).astype(F32)
        a1 = a0 - a_hi
        a_mid = a1.astype(BF16).astype(F32)
        a_lo = a1 - a_mid
        blk_f = (blk + jnp.zeros((TBLK, 128), I32)).astype(F32)
        payload = jnp.where(lane < ne, pos, jnp.where(lane < 2 * ne, blk_f, jnp.where(
            lane < 3 * ne, pltpu.roll(a_hi, 2 * ne, 1), jnp.where(
                lane < 4 * ne, pltpu.roll(a_mid, 3 * ne, 1),
                jnp.where(lane < 5 * ne, pltpu.roll(a_lo, 4 * ne, 1), 0.0))))).astype(BF16)
        for ex in range(ne):
            c0 = st_ref[ex, blk]
            base = pl.multiple_of(jnp.minimum(c0 & (-TBLK), CAP - win), TBLK)
            slot1 = slot_off + base.astype(F32)
            onehot = jnp.where(key_ref[ex:ex + 1, pl.ds(o, TBLK)] == slot1, 1.0, 0.0).astype(BF16)
            moved = _bdot(onehot, payload)
            rows = pl.ds(base, win)
            out_ref[rows, :] = out_ref[rows, :] + jnp.where(res_expert == ex, moved, 0.0)
        return carry

    lax.fori_loop(0, SEQ // TBLK, per_block, 0)


def _expert_choice(aff_t, aff):
    key, starts = pl.pallas_call(
        _select_body,
        grid=(BATCH,),
        in_specs=[pl.BlockSpec((None, N_EXPERTS, SEQ), lambda b: (b, 0, 0))],
        out_specs=[pl.BlockSpec((None, N_EXPERTS, SEQ), lambda b: (b, 0, 0)),
                   pl.BlockSpec((None, N_EXPERTS, 128), lambda b: (b, 0, 0))],
        out_shape=[jax.ShapeDtypeStruct((BATCH, N_EXPERTS, SEQ), F32),
                   jax.ShapeDtypeStruct((BATCH, N_EXPERTS, 128), I32)],
        compiler_params=_cparams(("arbitrary",)),
        name="expert_select",
    )(aff_t)
    return pl.pallas_call(
        _compact_body,
        grid=(BATCH,),
        in_specs=[pl.BlockSpec((None, N_EXPERTS, 128), lambda b: (b, 0, 0), memory_space=pltpu.SMEM),
                  pl.BlockSpec((None, N_EXPERTS, SEQ), lambda b: (b, 0, 0)),
                  pl.BlockSpec((None, SEQ, 128), lambda b: (b, 0, 0))],
        out_specs=pl.BlockSpec((None, CAP, 128), lambda b: (b, 0, 0)),
        out_shape=jax.ShapeDtypeStruct((BATCH, CAP, 128), F32),
        compiler_params=_cparams(("arbitrary",), VMEM_LIMIT),
        name="expert_compact",
    )(starts, key, aff)


HALF = D_MODEL // 2


def _gather_body(idx_ref, h_ref, o_ref, buf):
    sub = lax.broadcasted_iota(I32, (8, HALF), 0)

    def group(gi, carry):
        base = pl.multiple_of(gi * 8, 8)
        acc = jnp.zeros((8, HALF), F32)
        for j in range(8):
            row = idx_ref[0, base + j]
            tile = h_ref[pl.ds(pl.multiple_of((row >> 3) << 3, 8), 8), :]
            acc = jnp.where(sub == j, pltpu.roll(tile, (j - row) & 7, 0), acc)
        buf[pl.ds(base, 8), :] = acc
        return carry

    lax.fori_loop(0, CAP // 8, group, 0)
    o_ref[...] = buf[...].astype(BF16)


def _gather(idx3, h2):
    return pl.pallas_call(
        _gather_body,
        grid=(BATCH, 2, N_EXPERTS),
        in_specs=[pl.BlockSpec((None, 1, CAP), lambda b, dh, e: (b * N_EXPERTS + e, 0, 0),
                               memory_space=pltpu.SMEM),
                  pl.BlockSpec((None, SEQ, HALF), lambda b, dh, e: (b, 0, dh))],
        out_specs=pl.BlockSpec((None, None, CAP, HALF), lambda b, dh, e: (b, e, 0, dh)),
        out_shape=jax.ShapeDtypeStruct((BATCH, N_EXPERTS, CAP, D_MODEL), BF16),
        scratch_shapes=[pltpu.VMEM((CAP, HALF), F32)],
        compiler_params=_cparams(("arbitrary", "arbitrary", "arbitrary"), VMEM_LIMIT),
        name="moe_gather",
    )(idx3, h2)


def _ffn_body(x_ref, w1_ref, w3_ref, w2_ref, gt_ref, o_ref):
    f = pl.program_id(2)
    x = x_ref[...]
    a = jnp.dot(x, w1_ref[...].astype(BF16), preferred_element_type=F32)
    b = jnp.dot(x, w3_ref[...].astype(BF16), preferred_element_type=F32)
    hid = (a * _sigmoid(a) * b).astype(BF16)
    part = jnp.dot(hid, w2_ref[...].astype(BF16), preferred_element_type=F32)

    @pl.when(f == 0)
    def _():
        o_ref[...] = part

    @pl.when(f > 0)
    def _():
        o_ref[...] += part

    @pl.when(f == pl.num_programs(2) - 1)
    def _():
        o_ref[...] = o_ref[...] * gt_ref[...]


def _experts(xe, w1, w3, w2, gates4):
    tf = 1024
    return pl.pallas_call(
        _ffn_body,
        grid=(BATCH, N_EXPERTS, D_EXPERT // tf),
        in_specs=[pl.BlockSpec((None, None, CAP, D_MODEL), lambda b, e, f: (b, e, 0, 0)),
                  pl.BlockSpec((None, D_MODEL, tf), lambda b, e, f: (e, 0, f)),
                  pl.BlockSpec((None, D_MODEL, tf), lambda b, e, f: (e, 0, f)),
                  pl.BlockSpec((None, tf, D_MODEL), lambda b, e, f: (e, f, 0)),
                  pl.BlockSpec((None, None, CAP, 1), lambda b, e, f: (b, e, 0, 0))],
        out_specs=pl.BlockSpec((None, None, CAP, D_MODEL), lambda b, e, f: (b, e, 0, 0)),
        out_shape=jax.ShapeDtypeStruct((BATCH, N_EXPERTS, CAP, D_MODEL), F32),
        compiler_params=_cparams(("arbitrary", "arbitrary", "arbitrary"), VMEM_LIMIT),
        name="moe_experts",
    )(xe, w1, w3, w2, gates4)


def _scatter_body(idx_ref, ye_ref, o_ref):
    @pl.when(pl.program_id(2) == 0)
    def _():
        o_ref[...] = jnp.zeros_like(o_ref)

    sub = lax.broadcasted_iota(I32, (8, HALF), 0)

    def group(gi, carry):
        base = pl.multiple_of(gi * 8, 8)
        rows8 = ye_ref[pl.ds(base, 8), :]
        for j in range(8):
            row = idx_ref[0, base + j]
            tile = pl.ds(pl.multiple_of((row >> 3) << 3, 8), 8)
            add = jnp.where(sub == (row & 7), jnp.broadcast_to(rows8[j:j + 1, :], (8, HALF)), 0.0)
            o_ref[tile, :] = o_ref[tile, :] + add
        return carry

    lax.fori_loop(0, CAP // 8, group, 0)


def _scatter(idx3, ye):
    return pl.pallas_call(
        _scatter_body,
        grid=(BATCH, 2, N_EXPERTS),
        in_specs=[pl.BlockSpec((None, 1, CAP), lambda b, dh, e: (b * N_EXPERTS + e, 0, 0),
                               memory_space=pltpu.SMEM),
                  pl.BlockSpec((None, None, CAP, HALF), lambda b, dh, e: (b, e, 0, dh))],
        out_specs=pl.BlockSpec((None, SEQ, HALF), lambda b, dh, e: (b, 0, dh)),
        out_shape=jax.ShapeDtypeStruct((BATCH, SEQ, D_MODEL), F32),
        compiler_params=_cparams(("arbitrary", "arbitrary", "arbitrary"), VMEM_LIMIT),
        name="moe_scatter",
    )(idx3, ye)


def _final_body(x1_ref, moe_ref, m5_ref, n_ref, o_ref):
    o_ref[...] = x1_ref[...] + m5_ref[...] * (_rms(moe_ref[...]) * n_ref[...])


def _final(x1, moe, m5, n2post):
    tt = 512
    tok = pl.BlockSpec((None, tt, D_MODEL), lambda b, t: (b, t, 0))
    return pl.pallas_call(
        _final_body,
        grid=(BATCH, SEQ // tt),
        in_specs=[tok, tok, pl.BlockSpec((None, 1, D_MODEL), lambda b, t: (b, 0, 0)),
                  pl.BlockSpec((1, D_MODEL), lambda b, t: (0, 0))],
        out_specs=tok,
        out_shape=jax.ShapeDtypeStruct((BATCH, SEQ, D_MODEL), F32),
        compiler_params=_cparams(("arbitrary", "arbitrary")),
        name="moe_residual",
    )(x1, moe, m5, n2post.reshape(1, D_MODEL))


def _layer(x, c, ctx, c_ctx, p):
    c8 = jnp.zeros((8, D_MODEL), F32).at[:BATCH].set(c).at[BATCH].set(c_ctx)
    mod = _modulation(c8, p['mod_w'], p['mod_b'])
    m_lat = [mod[:BATCH, i * D_MODEL:(i + 1) * D_MODEL].reshape(BATCH, 1, D_MODEL) for i in range(6)]
    m_ctx = [jnp.broadcast_to(mod[BATCH, i * D_MODEL:(i + 1) * D_MODEL], (BATCH, 1, D_MODEL)) for i in range(2)]

    w_in = p['w_in'].astype(BF16)
    w_hy = w_in[:, :HY_COLS]
    w_rw = w_in[:, HY_COLS:HY_COLS + RW_COLS]
    w_gate = w_in[:, HY_COLS + RW_COLS:]
    rw_args = (p['rw_mu'], p['rw_w_up'], p['rw_a_up'], p['rw_w0'], p['rw_a0'], p['rw_g_up'],
               p['rw_k_k'], p['rw_k_a'], p['rw_r_k'])

    (ctx_cols,) = _in_projection(ctx, p['norm1_pre'], 1.0 + m_ctx[1], m_ctx[0], [w_rw], CTX_LEN, 1)
    cr, cv, ckk, ce, ckm, cb, _, _ = _rwkv_prep(ctx_cols, *rw_args)
    zero_state = jnp.zeros((2, BATCH, NQ, RW_HEAD, QW), F32)
    _, _, ctx_states = _rwkv_scan(cr, cv, ckk, ce, ckm, cb, zero_state)

    hy_cols, rw_cols, gate_cols = _in_projection(x, p['norm1_pre'], 1.0 + m_lat[1], m_lat[0],
                                                 [w_hy, w_rw, w_gate], 1024, 2, [F32, F32, BF16])
    out_hy = _hyena_branch(hy_cols, p['hy_conv_w'], p['hy_conv_b'], p['hy_ffn_w1'], p['hy_ffn_b1'],
                           p['hy_ffn_w2'], p['hy_ffn_b2'], p['hy_ffn_w3'], p['hy_freq'], p['hy_bias'])
    r, v, kk, e, km, bvec, bonus, g = _rwkv_prep(rw_cols, *rw_args)
    y_f, y_b, _ = _rwkv_scan(r, v, kk, e, km, bvec, ctx_states)

    x1, h2, aff, aff_t = _merge(out_hy, y_f, y_b, bonus, g, gate_cols, x, p['hy_proj'], p['rw_proj'], p['w_out'],
                                p['rw_ln_g'], p['rw_ln_b'], m_lat[2], p['norm1_post'], p['norm2_pre'],
                                1.0 + m_lat[4], m_lat[3], p['router_w'])
    picked = _expert_choice(aff_t, aff)
    ne = N_EXPERTS
    idx = (TBLK * picked[:, :, ne:2 * ne] + picked[:, :, :ne]).astype(I32)
    gates = (picked[:, :, 2 * ne:3 * ne] + picked[:, :, 3 * ne:4 * ne]) + picked[:, :, 4 * ne:5 * ne]
    idx3 = jnp.swapaxes(idx, 1, 2).reshape(BATCH * ne, 1, CAP)
    gates4 = jnp.swapaxes(gates, 1, 2).reshape(BATCH, ne, CAP, 1)
    xe = _gather(idx3, h2)
    ye = _experts(xe, p['exp_w1'], p['exp_w3'], p['exp_w2'], gates4)
    moe = _scatter(idx3, ye)
    return _final(x1, moe, m_lat[5], p['norm2_post'])


def kernel(x, c, ctx, c_ctx, mod_w, mod_b, norm1_pre, norm1_post, norm2_pre, norm2_post, w_in, hy_conv_w, hy_conv_b, hy_ffn_w1, hy_ffn_b1, hy_ffn_w2, hy_ffn_b2, hy_ffn_w3, hy_freq, hy_bias, hy_proj, rw_mu, rw_w0, rw_w_up, rw_a0, rw_a_up, rw_g_up, rw_k_k, rw_k_a, rw_r_k, rw_ln_g, rw_ln_b, rw_proj, w_out, router_w, exp_w1, exp_w3, exp_w2):
    names = ('mod_w', 'mod_b', 'norm1_pre', 'norm1_post', 'norm2_pre', 'norm2_post', 'w_in', 'hy_conv_w',
             'hy_conv_b', 'hy_ffn_w1', 'hy_ffn_b1', 'hy_ffn_w2', 'hy_ffn_b2', 'hy_ffn_w3', 'hy_freq', 'hy_bias',
             'hy_proj', 'rw_mu', 'rw_w0', 'rw_w_up', 'rw_a0', 'rw_a_up', 'rw_g_up', 'rw_k_k', 'rw_k_a', 'rw_r_k',
             'rw_ln_g', 'rw_ln_b', 'rw_proj', 'w_out', 'router_w', 'exp_w1', 'exp_w3', 'exp_w2')
    vals = (mod_w, mod_b, norm1_pre, norm1_post, norm2_pre, norm2_post, w_in, hy_conv_w, hy_conv_b, hy_ffn_w1,
            hy_ffn_b1, hy_ffn_w2, hy_ffn_b2, hy_ffn_w3, hy_freq, hy_bias, hy_proj, rw_mu, rw_w0, rw_w_up, rw_a0,
            rw_a_up, rw_g_up, rw_k_k, rw_k_a, rw_r_k, rw_ln_g, rw_ln_b, rw_proj, w_out, router_w, exp_w1, exp_w3,
            exp_w2)
    depth = mod_w.shape[0]
    assert depth == 1, "single-layer block: the context stream only provides scan start states"
    p = {n: a[0] for n, a in zip(names, vals)}
    return _layer(x, c, ctx, c_ctx, p)
```

```python
import functools
import math

import numpy as np
import jax
import jax.numpy as jnp
from jax import lax
from jax.experimental import pallas as pl
from jax.experimental.pallas import tpu as pltpu

F32 = jnp.float32
BF16 = jnp.bfloat16
I32 = jnp.int32
HI = lax.Precision.HIGHEST

D_MODEL = 1024
BATCH = 4
SEQ = 8192
CTX_LEN = 256

HY_WIDTH = 512
HY_BANDS = 16
HY_FILTER_ORDER = 64
HY_DECAY_TARGET = 1e-2
HY_DECAY_SHORT = 0.3
HY_DECAY_LONG = 1.5

RW_HEADS = 8
RW_HEAD = 64
RW_WIDTH = RW_HEADS * RW_HEAD
LORA_W = 64
LORA_A = 64
LORA_G = 128
RW_LN_EPS = 64e-5

HY_COLS = 3 * HY_WIDTH
RW_COLS = 3 * RW_WIDTH + LORA_W + LORA_A + LORA_G
GATE_COLS = 2 * D_MODEL

N_EXPERTS = 16
D_EXPERT = 2048
EC_CAPACITY = 2
NORM_EPS = 1e-6
CAP = EC_CAPACITY * SEQ // N_EXPERTS

NFFT = 2 * SEQ
FFT_R = 128
FFT_H = FFT_R // 2
FFT_J = 8
FFT_C = 512
CHUNK = 64
QUAD = 4
QW = QUAD * RW_HEAD
NQ = RW_HEADS // QUAD

VMEM_LIMIT = 56 * 1024 * 1024


def _cparams(sem, vmem=None, flags=None):
    return pltpu.CompilerParams(dimension_semantics=sem, vmem_limit_bytes=vmem, flags=flags)


def _dot_hi(a, b):
    return jnp.dot(a, b, precision=HI, preferred_element_type=F32)


def _bdot(a, b):
    return jnp.dot(a, b, preferred_element_type=F32)


def _split(a):
    hi = a.astype(BF16)
    return hi, (a - hi.astype(F32)).astype(BF16)


def _dot_c(m, x):
    return _bdot(m, x.astype(BF16))


def _dot_w3(x, w_hi, w_lo):
    xh, xl = _split(x)
    return _bdot(xh, w_hi) + (_bdot(xl, w_hi) + _bdot(xh, w_lo))


def _head_sums(x, ones_bf):
    xh, xl = _split(x)
    return _bdot(xh, ones_bf) + _bdot(xl, ones_bf)


def _rms(x):
    return x * lax.rsqrt(jnp.mean(x * x, axis=-1, keepdims=True) + NORM_EPS)


def _sigmoid(x):
    return 1.0 / (1.0 + jnp.exp(-x))


def _pad2(a, rows, cols):
    return jnp.zeros((rows, cols), F32).at[:a.shape[0], :a.shape[1]].set(a)


def _mod_body(c_ref, w_ref, b_ref, o_ref):
    c = c_ref[...]
    o_ref[...] = _dot_hi(c * _sigmoid(c), w_ref[...]) + b_ref[...]


def _modulation(c8, mod_w, mod_b):
    n = mod_w.shape[1]
    tn = 1536
    return pl.pallas_call(
        _mod_body,
        grid=(n // tn,),
        in_specs=[pl.BlockSpec((8, D_MODEL), lambda j: (0, 0)),
                  pl.BlockSpec((D_MODEL, tn), lambda j: (0, j)),
                  pl.BlockSpec((1, tn), lambda j: (0, j))],
        out_specs=pl.BlockSpec((8, tn), lambda j: (0, j)),
        out_shape=jax.ShapeDtypeStruct((8, n), F32),
        compiler_params=_cparams(("arbitrary",)),
        name="modulation",
    )(c8, mod_w, mod_b.reshape(1, n))


def _inproj_body(nw, x_ref, g_ref, sc_ref, sh_ref, *refs):
    w_refs, o_refs, h_ref = refs[:nw], refs[nw:2 * nw], refs[2 * nw]

    @pl.when(pl.program_id(2) == 0)
    def _():
        h = _rms(x_ref[...]) * g_ref[...] * sc_ref[...] + sh_ref[...]
        h_ref[...] = h.astype(BF16)

    h = h_ref[...]
    for w_ref, o_ref in zip(w_refs, o_refs):
        o_ref[...] = jnp.dot(h, w_ref[...], preferred_element_type=F32).astype(o_ref.dtype)


def _in_projection(x, g, scale1p, shift, weights, tm, nsplit, out_dtypes=None):
    out_dtypes = out_dtypes or [F32] * len(weights)
    bx, tx, _ = x.shape
    nw = len(weights)
    tns = [w.shape[1] // nsplit for w in weights]
    in_specs = [pl.BlockSpec((None, tm, D_MODEL), lambda b, m, n: (b, m, 0)),
                pl.BlockSpec((1, D_MODEL), lambda b, m, n: (0, 0)),
                pl.BlockSpec((None, 1, D_MODEL), lambda b, m, n: (b, 0, 0)),
                pl.BlockSpec((None, 1, D_MODEL), lambda b, m, n: (b, 0, 0))]
    in_specs += [pl.BlockSpec((D_MODEL, tn), lambda b, m, n: (0, n)) for tn in tns]
    out_specs = [pl.BlockSpec((None, tm, tn), lambda b, m, n: (b, m, n)) for tn in tns]
    out_shape = [jax.ShapeDtypeStruct((bx, tx, w.shape[1]), dt) for w, dt in zip(weights, out_dtypes)]
    return pl.pallas_call(
        functools.partial(_inproj_body, nw),
        grid=(bx, tx // tm, nsplit),
        in_specs=in_specs,
        out_specs=out_specs,
        out_shape=out_shape,
        scratch_shapes=[pltpu.VMEM((tm, D_MODEL), BF16)],
        compiler_params=_cparams(("arbitrary", "arbitrary", "arbitrary"), VMEM_LIMIT),
        name="in_projection",
    )(x, g.reshape(1, D_MODEL), scale1p, shift, *weights)


def _neighbours(cur, pv_ref, nx_ref):
    tt = cur.shape[0]
    t = pl.program_id(1)
    first = (t > 0).astype(F32)
    last = (t < pl.num_programs(1) - 1).astype(F32)
    prev_row = pv_ref[7:8, :] * first
    next_row = nx_ref[0:1, :] * last
    rows = lax.broadcasted_iota(I32, cur.shape, 0)
    prev = jnp.where(rows == 0, prev_row, pltpu.roll(cur, 1, 0))
    nxt = jnp.where(rows == tt - 1, next_row, pltpu.roll(cur, tt - 1, 0))
    return prev, nxt


def _halo_specs(tt, t_total, width):
    r8 = tt // 8
    last8 = t_total // 8 - 1
    return [pl.BlockSpec((None, tt, width), lambda b, t: (b, t, 0)),
            pl.BlockSpec((None, 8, width), lambda b, t: (b, jnp.maximum(t * r8 - 1, 0), 0)),
            pl.BlockSpec((None, 8, width), lambda b, t: (b, jnp.minimum((t + 1) * r8, last8), 0))]


def _hy_pre_body(cur_ref, pv_ref, nx_ref, w_ref, b_ref, z_ref, x0_ref):
    cur = cur_ref[...]
    prev, nxt = _neighbours(cur, pv_ref, nx_ref)
    u = w_ref[0:1, :] * prev + w_ref[1:2, :] * cur + w_ref[2:3, :] * nxt + b_ref[...]
    z_ref[...] = u[:, 2 * HY_WIDTH:] * u[:, HY_WIDTH:2 * HY_WIDTH]
    x0_ref[...] = u[:, :HY_WIDTH]


def _hyena_pre(hy_cols, conv_w, conv_b):
    tt = 512
    spec_o = pl.BlockSpec((None, tt, HY_WIDTH), lambda b, t: (b, t, 0))
    return pl.pallas_call(
        _hy_pre_body,
        grid=(BATCH, SEQ // tt),
        in_specs=_halo_specs(tt, SEQ, HY_COLS) + [
            pl.BlockSpec((3, HY_COLS), lambda b, t: (0, 0)),
            pl.BlockSpec((1, HY_COLS), lambda b, t: (0, 0))],
        out_specs=[spec_o, spec_o],
        out_shape=[jax.ShapeDtypeStruct((BATCH, SEQ, HY_WIDTH), F32)] * 2,
        compiler_params=_cparams(("arbitrary", "arbitrary"), VMEM_LIMIT),
        name="hyena_pre",
    )(hy_cols, hy_cols, hy_cols, conv_w, conv_b.reshape(1, HY_COLS))


def _filt_body(bands_ref, w1t_ref, w1c_ref, w1s_ref, b1_ref, w2_ref, b2_ref, w3_ref, fr_ref, dl_ref,
               f_ref, s_ref):
    i = pl.program_id(0)
    tt = f_ref.shape[0]
    rowi = lax.broadcasted_iota(I32, (tt, 1), 0) + i * tt
    rowf = rowi.astype(F32)
    tpos = rowf * (1.0 / (SEQ - 1))
    arg = (rowf * (2.0 * math.pi / SEQ)) * bands_ref[...]
    pre = tpos * w1t_ref[...] + _dot_hi(jnp.cos(arg), w1c_ref[...]) - _dot_hi(jnp.sin(arg), w1s_ref[...]) + b1_ref[...]
    freq = fr_ref[...]
    hid = jnp.sin(freq * pre)
    hid = jnp.sin(freq * (_dot_hi(hid, w2_ref[...]) + b2_ref[...]))
    filt = _dot_hi(hid, w3_ref[...]) * jnp.exp(-tpos * dl_ref[...])
    lane = lax.broadcasted_iota(I32, filt.shape, 1)
    rows = lax.broadcasted_iota(I32, filt.shape, 0) + i * tt
    filt = jnp.where((rows == 0) & (lane >= HY_WIDTH), 0.0, filt)
    f_ref[...] = filt

    @pl.when(i == 0)
    def _():
        s_ref[...] = jnp.zeros_like(s_ref)

    s_ref[...] += jnp.sum(jnp.abs(filt), axis=0, keepdims=True)


def _hyena_filter(w1, b1, w2, b2, w3, freq):
    tt = 512
    bands = np.zeros((1, 128), np.float32)
    bands[0, :HY_BANDS] = np.linspace(1e-4, HY_BANDS - 1, HY_BANDS, dtype=np.float32)
    deltas = np.abs(np.linspace(math.log(HY_DECAY_TARGET) / HY_DECAY_LONG,
                                math.log(HY_DECAY_TARGET) / HY_DECAY_SHORT, HY_WIDTH, dtype=np.float32))
    dl = np.concatenate([deltas, deltas])[None, :].astype(np.float32)
    args = (jnp.asarray(bands),
            _pad2(w1[0:1], 1, 128),
            _pad2(w1[1:1 + HY_BANDS], 128, 128),
            _pad2(w1[1 + HY_BANDS:], 128, 128),
            _pad2(b1[None], 1, 128),
            _pad2(w2, 128, 128),
            _pad2(b2[None], 1, 128),
            _pad2(w3, 128, 2 * HY_WIDTH),
            _pad2(freq[None], 1, 128),
            jnp.asarray(dl))
    full = lambda a: pl.BlockSpec(a.shape, lambda i: (0, 0))
    return pl.pallas_call(
        _filt_body,
        grid=(SEQ // tt,),
        in_specs=[full(a) for a in args],
        out_specs=[pl.BlockSpec((tt, 2 * HY_WIDTH), lambda i: (i, 0)),
                   pl.BlockSpec((1, 2 * HY_WIDTH), lambda i: (0, 0))],
        out_shape=[jax.ShapeDtypeStruct((SEQ, 2 * HY_WIDTH), F32),
                   jax.ShapeDtypeStruct((1, 2 * HY_WIDTH), F32)],
        compiler_params=_cparams(("arbitrary",)),
        name="hyena_filter",
    )(*args)


def _fft_tables():
    k = np.arange(FFT_R)
    n = np.arange(FFT_H)
    th = 2.0 * np.pi * np.outer(k, n) / FFT_R
    f1 = np.stack([np.cos(th), -np.sin(th)])
    eye = np.eye(FFT_J)
    g1 = np.einsum('pkn,jm->pkjnm', f1, eye).reshape(2 * FFT_R * FFT_J, FFT_H * FFT_J)
    f3 = np.stack([np.cos(th.T), -np.sin(th.T)]) / NFFT
    g3 = np.einsum('pnk,jm->njpkm', f3, eye).reshape(FFT_H * FFT_J, 2 * FFT_R * FFT_J)
    th2 = 2.0 * np.pi * np.outer(k, k) / FFT_R
    c, s = np.cos(th2), np.sin(th2)
    fc = np.block([[c, s], [-s, c]])
    fi = np.block([[c, -s], [s, c]])
    return tuple(jnp.asarray(a, F32) for a in (g1, g3, fc, fi))


def _twiddle_tables():
    k = np.arange(FFT_R)
    ang = 2.0 * np.pi * (np.outer(k, k) % NFFT) / NFFT
    first = ang.reshape(FFT_R, FFT_R // FFT_J, FFT_J).transpose(1, 0, 2).reshape(FFT_R // FFT_J, FFT_R * FFT_J, 1)
    mid = ang.reshape(FFT_R // MID_K, MID_K, FFT_R).transpose(0, 2, 1)
    return tuple(jnp.asarray(f(a), F32) for a in (first, mid) for f in (np.cos, np.sin))


def _fft1_body(z_ref, g_ref, cs_ref, sn_ref, o_ref):
    c = z_ref.shape[-1]
    z = z_ref[...].reshape(FFT_H * FFT_J, c)
    a = _dot_c(g_ref[...], z)
    half = FFT_R * FFT_J
    ar, ai = a[:half], a[half:]
    cs, sn = cs_ref[...], sn_ref[...]
    o_ref[0] = (ar * cs + ai * sn).reshape(FFT_R, FFT_J, c)
    o_ref[1] = (ai * cs - ar * sn).reshape(FFT_R, FFT_J, c)


def _fft_stage1(z4, g1, tw):
    bz, _, _, c = z4.shape
    tab = pl.BlockSpec(g1.shape, lambda b, n, cb: (0, 0))
    col = pl.BlockSpec((None, FFT_R * FFT_J, 1), lambda b, n, cb: (n, 0, 0))
    return pl.pallas_call(
        _fft1_body,
        grid=(bz, FFT_R // FFT_J, c // FFT_C),
        in_specs=[pl.BlockSpec((None, FFT_H, FFT_J, FFT_C), lambda b, n, cb: (b, 0, n, cb)), tab, col, col],
        out_specs=pl.BlockSpec((None, 2, FFT_R, FFT_J, FFT_C), lambda b, n, cb: (b, 0, 0, n, cb)),
        out_shape=jax.ShapeDtypeStruct((bz, 2, FFT_R, FFT_R, c), F32),
        compiler_params=_cparams(("arbitrary", "arbitrary", "arbitrary"), VMEM_LIMIT),
        name="fft_stage1",
    )(z4, g1, *tw)


MID_K = 4


def _filt_spec_body(a_ref, f_ref, h_ref):
    for i in range(MID_K):
        x = _dot_c(f_ref[...], jnp.concatenate([a_ref[0, i], a_ref[1, i]], axis=0))
        xr, xi = x[:FFT_R], x[FFT_R:]
        h_ref[i, 0] = xr[:, :HY_WIDTH] + xr[:, HY_WIDTH:]
        h_ref[i, 1] = xi[:, :HY_WIDTH] - xi[:, HY_WIDTH:]


def _filter_spectrum(a, fc):
    tab = pl.BlockSpec(fc.shape, lambda k: (0, 0))
    return pl.pallas_call(
        _filt_spec_body,
        grid=(FFT_R // MID_K,),
        in_specs=[pl.BlockSpec((None, 2, MID_K, FFT_R, 2 * HY_WIDTH), lambda k: (0, 0, k, 0, 0)), tab],
        out_specs=pl.BlockSpec((MID_K, 2, FFT_R, HY_WIDTH), lambda k: (k, 0, 0, 0)),
        out_shape=jax.ShapeDtypeStruct((FFT_R, 2, FFT_R, HY_WIDTH), F32),
        compiler_params=_cparams(("arbitrary",), VMEM_LIMIT),
        name="filter_spectrum",
    )(a, fc)


def _fft_mid_body(a_ref, h_ref, fc_ref, fi_ref, cs_ref, sn_ref, o_ref):
    for i in range(MID_K):
        x = _dot_c(fc_ref[...], jnp.concatenate([a_ref[0, i], a_ref[1, i]], axis=0))
        xr, xi = x[:FFT_R], x[FFT_R:]
        hr, him = h_ref[i, 0], h_ref[i, 1]
        yr = xr * hr - xi * him
        yi = xr * him + xi * hr
        p = _dot_c(fi_ref[...], jnp.concatenate([yr, yi], axis=0))
        pr, pim = p[:FFT_R], p[FFT_R:]
        cs, sn = cs_ref[:, i:i + 1], sn_ref[:, i:i + 1]
        o_ref[0, i] = pr * cs - pim * sn
        o_ref[1, i] = pr * sn + pim * cs


def _fft_mid(a, h, fc, fi, tw):
    bz = a.shape[0]
    blk = pl.BlockSpec((None, 2, MID_K, FFT_R, HY_WIDTH), lambda k, b: (b, 0, k, 0, 0))
    tab = pl.BlockSpec(fc.shape, lambda k, b: (0, 0))
    col = pl.BlockSpec((None, FFT_R, MID_K), lambda k, b: (k, 0, 0))
    return pl.pallas_call(
        _fft_mid_body,
        grid=(FFT_R // MID_K, bz),
        in_specs=[blk, pl.BlockSpec((MID_K, 2, FFT_R, HY_WIDTH), lambda k, b: (k, 0, 0, 0)), tab, tab, col, col],
        out_specs=blk,
        out_shape=jax.ShapeDtypeStruct(a.shape, F32),
        compiler_params=_cparams(("arbitrary", "arbitrary"), VMEM_LIMIT),
        name="fft_mid",
    )(a, h, fc, fi, *tw)


def _ifft_body(q_ref, g_ref, z_ref, x0_ref, inv_ref, hb_ref, o_ref):
    c = HY_WIDTH
    q = q_ref[...].reshape(2 * FFT_R * FFT_J, c)
    conv = _dot_c(g_ref[...], q) * inv_ref[...]
    z = z_ref[...].reshape(FFT_H * FFT_J, c)
    x0 = x0_ref[...].reshape(FFT_H * FFT_J, c)
    o_ref[...] = ((conv + hb_ref[...] * z) * x0).reshape(FFT_H, FFT_J, c)


def _ifft_gate(q, g3, z4, x04, inv_norm, hy_bias):
    row = pl.BlockSpec((None, FFT_H, FFT_J, HY_WIDTH), lambda b, n: (b, 0, n, 0))
    vec = pl.BlockSpec((1, HY_WIDTH), lambda b, n: (0, 0))
    tab = pl.BlockSpec(g3.shape, lambda b, n: (0, 0))
    return pl.pallas_call(
        _ifft_body,
        grid=(BATCH, FFT_R // FFT_J),
        in_specs=[pl.BlockSpec((None, 2, FFT_R, FFT_J, HY_WIDTH), lambda b, n: (b, 0, 0, n, 0)),
                  tab, row, row, vec, vec],
        out_specs=row,
        out_shape=jax.ShapeDtypeStruct(z4.shape, F32),
        compiler_params=_cparams(("arbitrary", "arbitrary"), VMEM_LIMIT),
        name="ifft_gate",
    )(q, g3, z4, x04, inv_norm, hy_bias)


def _hyena_branch(hy_cols, conv_w, conv_b, w1, b1, w2, b2, w3, freq, hy_bias):
    g1, g3, fc, fi = (t.astype(BF16) for t in _fft_tables())
    z, x0 = _hyena_pre(hy_cols, conv_w, conv_b)
    filt, l1 = _hyena_filter(w1, b1, w2, b2, w3, freq)
    inv_norm = 1.0 / (l1[:, :HY_WIDTH] + l1[:, HY_WIDTH:])
    tw1c, tw1s, tw2c, tw2s = _twiddle_tables()
    fa = _fft_stage1(filt.reshape(1, FFT_H, FFT_R, 2 * HY_WIDTH), g1, (tw1c, tw1s))
    h = _filter_spectrum(fa, fc)
    z4 = z.reshape(BATCH, FFT_H, FFT_R, HY_WIDTH)
    za = _fft_stage1(z4, g1, (tw1c, tw1s))
    q = _fft_mid(za, h, fc, fi, (tw2c, tw2s))
    out = _ifft_gate(q, g3, z4, x0.reshape(z4.shape), inv_norm, hy_bias.reshape(1, HY_WIDTH))
    return out.reshape(BATCH, SEQ, HY_WIDTH)


def _rw_prep_body(cur_ref, pv_ref, nx_ref, mu_ref, w2h_ref, w2l_ref, w0_ref, a0_ref, guph_ref, gupl_ref, kkw_ref,
                  ka_ref, rk_ref, ones_ref, r_o, v_o, kk_o, e_o, km_o, b_o, bonus_o, g_o):
    cur = cur_ref[...]
    prev, nxt = _neighbours(cur, pv_ref, nx_ref)
    z = cur + mu_ref[0:1, :] * (prev - cur) + mu_ref[1:2, :] * (nxt - cur)
    w_ = RW_WIDTH
    r, k, v = z[:, :w_], z[:, w_:2 * w_], z[:, 2 * w_:3 * w_]
    ll = z[:, 3 * w_:3 * w_ + 128]
    lg = z[:, 3 * w_ + 128:]
    lane = lax.broadcasted_iota(I32, ll.shape, 1)
    tl = jnp.where(lane < LORA_W, jnp.tanh(ll), ll)
    ones = ones_ref[...]
    kkr = k * kkw_ref[...]
    kk = kkr * lax.rsqrt(jnp.maximum(_head_sums(kkr * kkr, ones), 1e-24))
    bonus = jnp.zeros_like(r)
    for d in range(2):
        proj = _dot_w3(tl, w2h_ref[d], w2l_ref[d])
        wl = proj[:, :w_] + w0_ref[d:d + 1, :]
        al = proj[:, w_:] + a0_ref[d:d + 1, :]
        nwl = -wl
        w = -(jnp.maximum(nwl, 0.0) + jnp.log(1.0 + jnp.exp(-jnp.abs(nwl)))) - 0.5
        a = _sigmoid(al)
        km = k * (1.0 + (a - 1.0) * ka_ref[...])
        e_o[d] = jnp.exp(w)
        km_o[d] = km
        b_o[d] = kk * a
        bonus = bonus + _head_sums(r * km * rk_ref[...], ones)
    r_o[...] = r
    v_o[...] = v.astype(BF16)
    kk_o[...] = kk
    bonus_o[...] = (bonus * v).astype(BF16)
    g_o[...] = _dot_w3(_sigmoid(lg), guph_ref[...], gupl_ref[...]).astype(BF16)


def _head_ones():
    h = np.arange(RW_WIDTH) // RW_HEAD
    return jnp.asarray((h[:, None] == h[None, :]).astype(np.float32), BF16)


def _rwkv_prep(rw_cols, mu, w_up, a_up, w0, a0, g_up, k_k, k_a, r_k):
    bx, tx, _ = rw_cols.shape
    tt = min(512, tx)
    w2 = jnp.zeros((2, 128, 2 * RW_WIDTH), F32)
    w2 = w2.at[:, :LORA_W, :RW_WIDTH].set(w_up).at[:, LORA_W:, RW_WIDTH:].set(a_up)
    vec = lambda a: a.reshape(1, RW_WIDTH)
    full2 = lambda a: pl.BlockSpec(a.shape, lambda b, t: (0,) * a.ndim)
    args = (mu, *_split(w2), w0, a0, *_split(g_up), vec(k_k), vec(k_a), vec(r_k), _head_ones())
    one = pl.BlockSpec((None, tt, RW_WIDTH), lambda b, t: (b, t, 0))
    two = pl.BlockSpec((2, None, tt, RW_WIDTH), lambda b, t: (0, b, t, 0))
    s1 = jax.ShapeDtypeStruct((bx, tx, RW_WIDTH), F32)
    s1h = jax.ShapeDtypeStruct((bx, tx, RW_WIDTH), BF16)
    s2 = jax.ShapeDtypeStruct((2, bx, tx, RW_WIDTH), F32)
    return pl.pallas_call(
        _rw_prep_body,
        grid=(bx, tx // tt),
        in_specs=_halo_specs(tt, tx, RW_COLS) + [full2(a) for a in args],
        out_specs=[one, one, one, two, two, two, one, one],
        out_shape=[s1, s1h, s1, s2, s2, s2, s1h, s1h],
        compiler_params=_cparams(("arbitrary", "arbitrary"), VMEM_LIMIT),
        name="rwkv_prep",
    )(rw_cols, rw_cols, rw_cols, *args)


def _chunk_cumsum(e, fwd):
    row = lax.broadcasted_iota(I32, e.shape, 0)
    c = e
    s = 1
    while s < CHUNK:
        if fwd:
            c = c + jnp.where(row >= s, pltpu.roll(c, s, 0), 0.0)
        else:
            c = c + jnp.where(row < CHUNK - s, pltpu.roll(c, CHUNK - s, 0), 0.0)
        s *= 2
    return c


def _block_diag(x4, lane_head):
    return jnp.concatenate([jnp.where(lane_head == h, x4, 0.0) for h in range(QUAD)], axis=0).astype(BF16)


def _mm(a, b):
    return jnp.dot(a.astype(BF16), b, preferred_element_type=F32)


def _mm_nt(a, b):
    return lax.dot_general(a.astype(BF16), b, (((1,), (1,)), ((), ())), preferred_element_type=F32)


def _scan_chain(fwd, r, v, kk, e, km, bv, s_ref, y_ref):
    n = CHUNK
    cum = _chunk_cumsum(e, fwd)
    tot = jnp.sum(e, axis=0, keepdims=True)
    grow = jnp.exp(cum)
    at = -kk * jnp.exp(e - cum)
    bt = bv * grow
    kt = km * grow
    rt = r * jnp.exp(-cum)
    tail = jnp.exp(cum - tot)
    bb = bv * tail
    kb = km * tail
    p_all = jnp.exp(-tot)

    row = lax.broadcasted_iota(I32, (n, QW), 0)
    lane = lax.broadcasted_iota(I32, (n, QW), 1)
    col = lane & (RW_HEAD - 1)
    lane_head = lane >> 6
    incl = (row >= col) if fwd else (row <= col)
    strict = (row > col) if fwd else (row < col)
    eye = jnp.where(row == col, 1.0, 0.0)
    bd = lambda x4: _block_diag(x4, lane_head)

    def quad_steps(q):
        ql = slice(QW * q, QW * (q + 1))
        a4, v4 = at[:, ql], v[:, ql]
        ar = jnp.concatenate([a4, rt[:, ql]], axis=0)
        bd_v = bd(v4)
        gb = _mm_nt(ar, bd(bt[:, ql]))
        gk = _mm_nt(ar, bd(kt[:, ql]))
        yield
        m = jnp.where(strict, gb[:n], 0.0)
        g_rb = jnp.where(incl, gb[n:], 0.0)
        g_rk = jnp.where(incl, gk[n:], 0.0)
        w4 = _mm(jnp.where(strict, gk[:n], 0.0), bd_v)
        t4 = eye + m
        p = _mm(m, bd(m))
        s4 = s_ref[q]
        sa = _mm_nt(ar, bd(s4))
        yield
        for level in range(1, 6):
            bd_p = bd(p)
            if level < 5:
                both = _mm(jnp.concatenate([p, t4], axis=0), bd_p)
                p = both[:n]
                t4 = t4 + both[n:]
            else:
                t4 = t4 + _mm(t4, bd_p)
            yield
        u4 = _mm(t4, bd(sa[:n] + w4))
        yield
        y_ref[:, ql] = sa[n:] + _mm(g_rb, bd(u4)) + _mm(g_rk, bd_v)
        uv = jnp.concatenate([u4, v4], axis=0).astype(BF16)
        bk = jnp.concatenate([bb[:, ql], kb[:, ql]], axis=0).astype(BF16)
        full = lax.dot_general(uv, bk, (((0,), (0,)), ((), ())), preferred_element_type=F32)
        upd = s4 * p_all[:, ql]
        for h in range(QUAD):
            upd = upd + jnp.where(lane_head == h, full[RW_HEAD * h:RW_HEAD * (h + 1)], 0.0)
        s_ref[q] = upd

    return [quad_steps(q) for q in range(NQ)]


def _round_robin(gens):
    gens = list(gens)
    while gens:
        for g in list(gens):
            try:
                next(g)
            except StopIteration:
                gens.remove(g)


def _rw_scan_body(nb, rf_ref, vf_ref, kf_ref, rb_ref, vb_ref, kb_ref, ef_ref, mf_ref, bf_ref, eb_ref, mb_ref, bb_ref,
                  s0_ref, yf_ref, yb_ref, sf_ref, s_scr):
    c = pl.program_id(1)

    @pl.when(c == 0)
    def _():
        s_scr[...] = s0_ref[...]

    steps = []
    for i in range(nb):
        steps += _scan_chain(True, rf_ref[i], vf_ref[i].astype(F32), kf_ref[i], ef_ref[i], mf_ref[i], bf_ref[i],
                             s_scr.at[0, i], yf_ref.at[i])
        steps += _scan_chain(False, rb_ref[i], vb_ref[i].astype(F32), kb_ref[i], eb_ref[i], mb_ref[i], bb_ref[i],
                             s_scr.at[1, i], yb_ref.at[i])
    _round_robin(steps)

    @pl.when(c == pl.num_programs(1) - 1)
    def _():
        sf_ref[...] = s_scr[...]


def _rwkv_scan(r, v, kk, e, km, bvec, s0):
    bx, tx, _ = r.shape
    nc = tx // CHUNK
    nb = 4
    fpos = lambda g, c: (g, c, 0)
    bpos = lambda g, c: (g, nc - 1 - c, 0)
    tok = lambda f: pl.BlockSpec((nb, CHUNK, RW_WIDTH), f)
    tokd = lambda d, f: pl.BlockSpec((None, nb, CHUNK, RW_WIDTH), lambda g, c: (d,) + f(g, c))
    st = pl.BlockSpec((2, nb, NQ, RW_HEAD, QW), lambda g, c: (0, g, 0, 0, 0))
    y_shape = jax.ShapeDtypeStruct((bx, tx, RW_WIDTH), F32)
    return pl.pallas_call(
        functools.partial(_rw_scan_body, nb),
        grid=(bx // nb, nc),
        in_specs=[tok(fpos)] * 3 + [tok(bpos)] * 3 + [tokd(0, fpos)] * 3 + [tokd(1, bpos)] * 3 + [st],
        out_specs=[tok(fpos), tok(bpos), st],
        out_shape=[y_shape, y_shape, jax.ShapeDtypeStruct((2, bx, NQ, RW_HEAD, QW), F32)],
        scratch_shapes=[pltpu.VMEM((2, nb, NQ, RW_HEAD, QW), F32)],
        compiler_params=_cparams(("arbitrary", "arbitrary"), VMEM_LIMIT),
        name="rwkv_scan",
    )(r, v, kk, r, v, kk, e, km, bvec, e, km, bvec, s0)


def _merge_body(hy_ref, yf_ref, yb_ref, bonus_ref, g_ref, gate_ref, x_ref, hyp_ref, rwp_ref, wo_ref, ones_ref,
                lng_ref, lnb_ref, m2_ref, n1_ref, n2_ref, sc4_ref, m3_ref, rwh_ref, rwl_ref,
                x1_ref, h2_ref, aff_ref, afft_ref):
    ones = ones_ref[...]
    ysum = yf_ref[...] + yb_ref[...]
    mean = _head_sums(ysum, ones) * (1.0 / RW_HEAD)
    dev = ysum - mean
    var = _head_sums(dev * dev, ones) * (1.0 / RW_HEAD)
    yn = dev * lax.rsqrt(var + RW_LN_EPS) * lng_ref[...] + lnb_ref[...]
    y_rw = (yn + bonus_ref[...].astype(F32)) * g_ref[...].astype(F32)
    gates = _sigmoid(gate_ref[...].astype(F32))
    mix = (gates[:, :D_MODEL] * _bdot(hy_ref[...].astype(BF16), hyp_ref[...])
           + gates[:, D_MODEL:] * _bdot(y_rw.astype(BF16), rwp_ref[...]))
    mixed = _bdot(mix.astype(BF16), wo_ref[...])
    x1 = x_ref[...] + m2_ref[...] * (_rms(mixed) * n1_ref[...])
    x1_ref[...] = x1
    h2 = _rms(x1) * n2_ref[...] * sc4_ref[...] + m3_ref[...]
    h2_ref[...] = h2
    logits = _dot_w3(h2, rwh_ref[...], rwl_ref[...])
    lane = lax.broadcasted_iota(I32, logits.shape, 1)
    logits = jnp.where(lane < N_EXPERTS, logits, -1e30)
    ex = jnp.exp(logits - jnp.max(logits, axis=1, keepdims=True))
    aff = ex / jnp.sum(ex, axis=1, keepdims=True)
    aff_ref[...] = aff
    afft_ref[...] = jnp.transpose(aff)[:N_EXPERTS, :]


def _merge(out_hy, y_f, y_b, bonus, g, gate_cols, x, hy_proj, rw_proj, w_out, ln_g, ln_b, m2, n1post, n2pre, sc4, m3,
           router_w):
    tt = 256
    tok = lambda w: pl.BlockSpec((None, tt, w), lambda b, t: (b, t, 0))
    full = lambda a: pl.BlockSpec(a.shape, lambda b, t: (0,) * a.ndim)
    per_b = pl.BlockSpec((None, 1, D_MODEL), lambda b, t: (b, 0, 0))
    vec = lambda a, n: a.reshape(1, n)
    consts = (hy_proj.astype(BF16), rw_proj.astype(BF16), w_out.astype(BF16), _head_ones(),
              vec(ln_g, RW_WIDTH), vec(ln_b, RW_WIDTH))
    router = _split(_pad2(router_w, D_MODEL, 128))
    return pl.pallas_call(
        _merge_body,
        grid=(BATCH, SEQ // tt),
        in_specs=[tok(HY_WIDTH), tok(RW_WIDTH), tok(RW_WIDTH), tok(RW_WIDTH), tok(RW_WIDTH), tok(GATE_COLS),
                  tok(D_MODEL)]
                 + [full(a) for a in consts]
                 + [per_b, full(vec(n1post, D_MODEL)), full(vec(n2pre, D_MODEL)), per_b, per_b,
                    full(router[0]), full(router[1])],
        out_specs=[tok(D_MODEL), tok(D_MODEL), tok(128),
                   pl.BlockSpec((None, N_EXPERTS, tt), lambda b, t: (b, 0, t))],
        out_shape=[jax.ShapeDtypeStruct((BATCH, SEQ, D_MODEL), F32),
                   jax.ShapeDtypeStruct((BATCH, SEQ, D_MODEL), F32),
                   jax.ShapeDtypeStruct((BATCH, SEQ, 128), F32),
                   jax.ShapeDtypeStruct((BATCH, N_EXPERTS, SEQ), F32)],
        compiler_params=_cparams(("arbitrary", "arbitrary"), VMEM_LIMIT),
        name="merge_router",
    )(out_hy, y_f, y_b, bonus, g, gate_cols, x, *consts, m2, vec(n1post, D_MODEL), vec(n2pre, D_MODEL), sc4, m3,
      *router)


TBLK = 128


def _select_body(afft_ref, key_ref, st_ref):
    ne = N_EXPERTS

    def bisect(_, carry):
        lo, hi = carry
        mid = lo + ((hi - lo) >> 1)
        n_ge = jnp.sum(jnp.where(afft_ref[...] >= pltpu.bitcast(mid, F32), 1.0, 0.0), axis=1, keepdims=True)
        ok = n_ge >= float(CAP)
        return jnp.where(ok, mid, lo), jnp.where(ok, hi, mid)

    lo0 = jnp.zeros((ne, 1), I32)
    hi0 = jnp.full((ne, 1), 0x7F800000, I32)
    thr_bits, _ = lax.fori_loop(0, 32, bisect, (lo0, hi0))
    thr = pltpu.bitcast(thr_bits, F32)
    need = float(CAP) - jnp.sum(jnp.where(afft_ref[...] > thr, 1.0, 0.0), axis=1, keepdims=True)

    ri = lax.broadcasted_iota(I32, (128, 128), 0)
    ci = lax.broadcasted_iota(I32, (128, 128), 1)
    upper = jnp.where(ri <= ci, 1.0, 0.0).astype(BF16)
    off_eq = jnp.zeros((ne, 1), F32)
    off_sel = jnp.zeros((ne, 1), F32)
    lane = lax.broadcasted_iota(I32, (ne, 128), 1)
    starts = jnp.zeros((ne, 128), F32)
    for blk in range(SEQ // TBLK):
        sl = slice(TBLK * blk, TBLK * (blk + 1))
        a_b = afft_ref[:, sl]
        eq_b = jnp.where(a_b == thr, 1.0, 0.0)
        inc = _bdot(eq_b.astype(BF16), upper) + off_eq
        off_eq = inc[:, 127:128]
        sel_b = jnp.where(a_b > thr, 1.0, 0.0) + eq_b * jnp.where(inc - eq_b < need, 1.0, 0.0)
        starts = jnp.where(lane == blk, off_sel, starts)
        cnt = _bdot(sel_b.astype(BF16), upper) + off_sel
        off_sel = cnt[:, 127:128]
        key_ref[:, sl] = cnt * sel_b
    st_ref[...] = starts.astype(I32)


def _compact_body(st_ref, key_ref, aff_ref, out_ref):
    ne = N_EXPERTS
    win = 2 * TBLK
    out_ref[...] = jnp.zeros_like(out_ref)
    lane = lax.broadcasted_iota(I32, (TBLK, 128), 1)
    pos = lax.broadcasted_iota(I32, (TBLK, 128), 0).astype(F32)
    slot_off = lax.broadcasted_iota(I32, (win, TBLK), 0).astype(F32) + 1.0
    res_expert = lax.broadcasted_iota(I32, (win, 128), 1) & (ne - 1)

    def per_block(blk, carry):
        o = pl.multiple_of(blk * TBLK, TBLK)
        a0 = aff_ref[pl.ds(o, TBLK), :]
        a_hi = a0.astype(BF16).astype(F32)
        a1 = a0 - a_hi
        a_mid = a1.astype(BF16).astype(F32)
        a_lo = a1 - a_mid
        blk_f = (blk + jnp.zeros((TBLK, 128), I32)).astype(F32)
        payload = jnp.where(lane < ne, pos, jnp.where(lane < 2 * ne, blk_f, jnp.where(
            lane < 3 * ne, pltpu.roll(a_hi, 2 * ne, 1), jnp.where(
                lane < 4 * ne, pltpu.roll(a_mid, 3 * ne, 1),
                jnp.where(lane < 5 * ne, pltpu.roll(a_lo, 4 * ne, 1), 0.0))))).astype(BF16)
        for ex in range(ne):
            c0 = st_ref[ex, blk]
            base = pl.multiple_of(jnp.minimum(c0 & (-TBLK), CAP - win), TBLK)
            slot1 = slot_off + base.astype(F32)
            onehot = jnp.where(key_ref[ex:ex + 1, pl.ds(o, TBLK)] == slot1, 1.0, 0.0).astype(BF16)
            moved = _bdot(onehot, payload)
            rows = pl.ds(base, win)
            out_ref[rows, :] = out_ref[rows, :] + jnp.where(res_expert == ex, moved, 0.0)
        return carry

    lax.fori_loop(0, SEQ // TBLK, per_block, 0)


def _expert_choice(aff_t, aff):
    key, starts = pl.pallas_call(
        _select_body,
        grid=(BATCH,),
        in_specs=[pl.BlockSpec((None, N_EXPERTS, SEQ), lambda b: (b, 0, 0))],
        out_specs=[pl.BlockSpec((None, N_EXPERTS, SEQ), lambda b: (b, 0, 0)),
                   pl.BlockSpec((None, N_EXPERTS, 128), lambda b: (b, 0, 0))],
        out_shape=[jax.ShapeDtypeStruct((BATCH, N_EXPERTS, SEQ), F32),
                   jax.ShapeDtypeStruct((BATCH, N_EXPERTS, 128), I32)],
        compiler_params=_cparams(("arbitrary",)),
        name="expert_select",
    )(aff_t)
    return pl.pallas_call(
        _compact_body,
        grid=(BATCH,),
        in_specs=[pl.BlockSpec((None, N_EXPERTS, 128), lambda b: (b, 0, 0), memory_space=pltpu.SMEM),
                  pl.BlockSpec((None, N_EXPERTS, SEQ), lambda b: (b, 0, 0)),
                  pl.BlockSpec((None, SEQ, 128), lambda b: (b, 0, 0))],
        out_specs=pl.BlockSpec((None, CAP, 128), lambda b: (b, 0, 0)),
        out_shape=jax.ShapeDtypeStruct((BATCH, CAP, 128), F32),
        compiler_params=_cparams(("arbitrary",), VMEM_LIMIT),
        name="expert_compact",
    )(starts, key, aff)


def _gather_body(idx_ref, h_ref, o_ref, buf):
    sub = lax.broadcasted_iota(I32, (8, D_MODEL), 0)

    def group(gi, carry):
        base = pl.multiple_of(gi * 8, 8)
        acc = jnp.zeros((8, D_MODEL), F32)
        for j in range(8):
            row = idx_ref[0, base + j]
            tile = h_ref[pl.ds(pl.multiple_of((row >> 3) << 3, 8), 8), :]
            acc = jnp.where(sub == j, pltpu.roll(tile, (j - row) & 7, 0), acc)
        buf[pl.ds(base, 8), :] = acc
        return carry

    lax.fori_loop(0, CAP // 8, group, 0)
    o_ref[...] = buf[...].astype(BF16)


def _gather(idx3, h2):
    return pl.pallas_call(
        _gather_body,
        grid=(BATCH, N_EXPERTS),
        in_specs=[pl.BlockSpec((None, 1, CAP), lambda b, e: (b * N_EXPERTS + e, 0, 0), memory_space=pltpu.SMEM),
                  pl.BlockSpec((None, SEQ, D_MODEL), lambda b, e: (b, 0, 0), pipeline_mode=pl.Buffered(1))],
        out_specs=pl.BlockSpec((None, None, CAP, D_MODEL), lambda b, e: (b, e, 0, 0)),
        out_shape=jax.ShapeDtypeStruct((BATCH, N_EXPERTS, CAP, D_MODEL), BF16),
        scratch_shapes=[pltpu.VMEM((CAP, D_MODEL), F32)],
        compiler_params=_cparams(("arbitrary", "arbitrary"), VMEM_LIMIT),
        name="moe_gather",
    )(idx3, h2)


def _ffn_body(x_ref, w1_ref, w3_ref, w2_ref, gt_ref, o_ref):
    f = pl.program_id(2)
    x = x_ref[...]
    a = jnp.dot(x, w1_ref[...].astype(BF16), preferred_element_type=F32)
    b = jnp.dot(x, w3_ref[...].astype(BF16), preferred_element_type=F32)
    hid = (a * _sigmoid(a) * b).astype(BF16)
    part = jnp.dot(hid, w2_ref[...].astype(BF16), preferred_element_type=F32)

    @pl.when(f == 0)
    def _():
        o_ref[...] = part

    @pl.when(f > 0)
    def _():
        o_ref[...] += part

    @pl.when(f == pl.num_programs(2) - 1)
    def _():
        o_ref[...] = o_ref[...] * gt_ref[...]


def _experts(xe, w1, w3, w2, gates4):
    tf = 1024
    return pl.pallas_call(
        _ffn_body,
        grid=(BATCH, N_EXPERTS, D_EXPERT // tf),
        in_specs=[pl.BlockSpec((None, None, CAP, D_MODEL), lambda b, e, f: (b, e, 0, 0)),
                  pl.BlockSpec((None, D_MODEL, tf), lambda b, e, f: (e, 0, f)),
                  pl.BlockSpec((None, D_MODEL, tf), lambda b, e, f: (e, 0, f)),
                  pl.BlockSpec((None, tf, D_MODEL), lambda b, e, f: (e, f, 0)),
                  pl.BlockSpec((None, None, CAP, 1), lambda b, e, f: (b, e, 0, 0))],
        out_specs=pl.BlockSpec((None, None, CAP, D_MODEL), lambda b, e, f: (b, e, 0, 0)),
        out_shape=jax.ShapeDtypeStruct((BATCH, N_EXPERTS, CAP, D_MODEL), F32),
        compiler_params=_cparams(("arbitrary", "arbitrary", "arbitrary"), VMEM_LIMIT),
        name="moe_experts",
    )(xe, w1, w3, w2, gates4)


def _scatter_body(idx_ref, ye_ref, o_ref):
    @pl.when(pl.program_id(1) == 0)
    def _():
        o_ref[...] = jnp.zeros_like(o_ref)

    sub = lax.broadcasted_iota(I32, (8, D_MODEL), 0)

    def group(gi, carry):
        base = pl.multiple_of(gi * 8, 8)
        rows8 = ye_ref[pl.ds(base, 8), :]
        for j in range(8):
            row = idx_ref[0, base + j]
            tile = pl.ds(pl.multiple_of((row >> 3) << 3, 8), 8)
            add = jnp.where(sub == (row & 7), jnp.broadcast_to(rows8[j:j + 1, :], (8, D_MODEL)), 0.0)
            o_ref[tile, :] = o_ref[tile, :] + add
        return carry

    lax.fori_loop(0, CAP // 8, group, 0)


def _scatter(idx3, ye):
    return pl.pallas_call(
        _scatter_body,
        grid=(BATCH, N_EXPERTS),
        in_specs=[pl.BlockSpec((None, 1, CAP), lambda b, e: (b * N_EXPERTS + e, 0, 0), memory_space=pltpu.SMEM),
                  pl.BlockSpec((None, None, CAP, D_MODEL), lambda b, e: (b, e, 0, 0))],
        out_specs=pl.BlockSpec((None, SEQ, D_MODEL), lambda b, e: (b, 0, 0), pipeline_mode=pl.Buffered(1)),
        out_shape=jax.ShapeDtypeStruct((BATCH, SEQ, D_MODEL), F32),
        compiler_params=_cparams(("arbitrary", "arbitrary"), VMEM_LIMIT),
        name="moe_scatter",
    )(idx3, ye)


def _final_body(x1_ref, moe_ref, m5_ref, n_ref, o_ref):
    o_ref[...] = x1_ref[...] + m5_ref[...] * (_rms(moe_ref[...]) * n_ref[...])


def _final(x1, moe, m5, n2post):
    tt = 512
    tok = pl.BlockSpec((None, tt, D_MODEL), lambda b, t: (b, t, 0))
    return pl.pallas_call(
        _final_body,
        grid=(BATCH, SEQ // tt),
        in_specs=[tok, tok, pl.BlockSpec((None, 1, D_MODEL), lambda b, t: (b, 0, 0)),
                  pl.BlockSpec((1, D_MODEL), lambda b, t: (0, 0))],
        out_specs=tok,
        out_shape=jax.ShapeDtypeStruct((BATCH, SEQ, D_MODEL), F32),
        compiler_params=_cparams(("arbitrary", "arbitrary")),
        name="moe_residual",
    )(x1, moe, m5, n2post.reshape(1, D_MODEL))


def _layer(x, c, ctx, c_ctx, p):
    c8 = jnp.zeros((8, D_MODEL), F32).at[:BATCH].set(c).at[BATCH].set(c_ctx)
    mod = _modulation(c8, p['mod_w'], p['mod_b'])
    m_lat = [mod[:BATCH, i * D_MODEL:(i + 1) * D_MODEL].reshape(BATCH, 1, D_MODEL) for i in range(6)]
    m_ctx = [jnp.broadcast_to(mod[BATCH, i * D_MODEL:(i + 1) * D_MODEL], (BATCH, 1, D_MODEL)) for i in range(2)]

    w_in = p['w_in'].astype(BF16)
    w_hy = w_in[:, :HY_COLS]
    w_rw = w_in[:, HY_COLS:HY_COLS + RW_COLS]
    w_gate = w_in[:, HY_COLS + RW_COLS:]
    rw_args = (p['rw_mu'], p['rw_w_up'], p['rw_a_up'], p['rw_w0'], p['rw_a0'], p['rw_g_up'],
               p['rw_k_k'], p['rw_k_a'], p['rw_r_k'])

    (ctx_cols,) = _in_projection(ctx, p['norm1_pre'], 1.0 + m_ctx[1], m_ctx[0], [w_rw], CTX_LEN, 1)
    cr, cv, ckk, ce, ckm, cb, _, _ = _rwkv_prep(ctx_cols, *rw_args)
    zero_state = jnp.zeros((2, BATCH, NQ, RW_HEAD, QW), F32)
    _, _, ctx_states = _rwkv_scan(cr, cv, ckk, ce, ckm, cb, zero_state)

    hy_cols, rw_cols, gate_cols = _in_projection(x, p['norm1_pre'], 1.0 + m_lat[1], m_lat[0],
                                                 [w_hy, w_rw, w_gate], 1024, 2, [F32, F32, BF16])
    out_hy = _hyena_branch(hy_cols, p['hy_conv_w'], p['hy_conv_b'], p['hy_ffn_w1'], p['hy_ffn_b1'],
                           p['hy_ffn_w2'], p['hy_ffn_b2'], p['hy_ffn_w3'], p['hy_freq'], p['hy_bias'])
    r, v, kk, e, km, bvec, bonus, g = _rwkv_prep(rw_cols, *rw_args)
    y_f, y_b, _ = _rwkv_scan(r, v, kk, e, km, bvec, ctx_states)

    x1, h2, aff, aff_t = _merge(out_hy, y_f, y_b, bonus, g, gate_cols, x, p['hy_proj'], p['rw_proj'], p['w_out'],
                                p['rw_ln_g'], p['rw_ln_b'], m_lat[2], p['norm1_post'], p['norm2_pre'],
                                1.0 + m_lat[4], m_lat[3], p['router_w'])
    picked = _expert_choice(aff_t, aff)
    ne = N_EXPERTS
    idx = (TBLK * picked[:, :, ne:2 * ne] + picked[:, :, :ne]).astype(I32)
    gates = (picked[:, :, 2 * ne:3 * ne] + picked[:, :, 3 * ne:4 * ne]) + picked[:, :, 4 * ne:5 * ne]
    idx3 = jnp.swapaxes(idx, 1, 2).reshape(BATCH * ne, 1, CAP)
    gates4 = jnp.swapaxes(gates, 1, 2).reshape(BATCH, ne, CAP, 1)
    xe = _gather(idx3, h2)
    ye = _experts(xe, p['exp_w1'], p['exp_w3'], p['exp_w2'], gates4)
    moe = _scatter(idx3, ye)
    return _final(x1, moe, m_lat[5], p['norm2_post'])


def kernel(x, c, ctx, c_ctx, mod_w, mod_b, norm1_pre, norm1_post, norm2_pre, norm2_post, w_in, hy_conv_w, hy_conv_b, hy_ffn_w1, hy_ffn_b1, hy_ffn_w2, hy_ffn_b2, hy_ffn_w3, hy_freq, hy_bias, hy_proj, rw_mu, rw_w0, rw_w_up, rw_a0, rw_a_up, rw_g_up, rw_k_k, rw_k_a, rw_r_k, rw_ln_g, rw_ln_b, rw_proj, w_out, router_w, exp_w1, exp_w3, exp_w2):
    names = ('mod_w', 'mod_b', 'norm1_pre', 'norm1_post', 'norm2_pre', 'norm2_post', 'w_in', 'hy_conv_w',
             'hy_conv_b', 'hy_ffn_w1', 'hy_ffn_b1', 'hy_ffn_w2', 'hy_ffn_b2', 'hy_ffn_w3', 'hy_freq', 'hy_bias',
             'hy_proj', 'rw_mu', 'rw_w0', 'rw_w_up', 'rw_a0', 'rw_a_up', 'rw_g_up', 'rw_k_k', 'rw_k_a', 'rw_r_k',
             'rw_ln_g', 'rw_ln_b', 'rw_proj', 'w_out', 'router_w', 'exp_w1', 'exp_w3', 'exp_w2')
    vals = (mod_w, mod_b, norm1_pre, norm1_post, norm2_pre, norm2_post, w_in, hy_conv_w, hy_conv_b, hy_ffn_w1,
            hy_ffn_b1, hy_ffn_w2, hy_ffn_b2, hy_ffn_w3, hy_freq, hy_bias, hy_proj, rw_mu, rw_w0, rw_w_up, rw_a0,
            rw_a_up, rw_g_up, rw_k_k, rw_k_a, rw_r_k, rw_ln_g, rw_ln_b, rw_proj, w_out, router_w, exp_w1, exp_w3,
            exp_w2)
    depth = mod_w.shape[0]
    assert depth == 1, "single-layer block: the context stream only provides scan start states"
    p = {n: a[0] for n, a in zip(names, vals)}
    return _layer(x, c, ctx, c_ctx, p)
```

```python
import functools
import math

import numpy as np
import jax
import jax.numpy as jnp
from jax import lax
from jax.experimental import pallas as pl
from jax.experimental.pallas import tpu as pltpu

F32 = jnp.float32
BF16 = jnp.bfloat16
I32 = jnp.int32
HI = lax.Precision.HIGHEST

D_MODEL = 1024
BATCH = 4
SEQ = 8192
CTX_LEN = 256

HY_WIDTH = 512
HY_BANDS = 16
HY_FILTER_ORDER = 64
HY_DECAY_TARGET = 1e-2
HY_DECAY_SHORT = 0.3
HY_DECAY_LONG = 1.5

RW_HEADS = 8
RW_HEAD = 64
RW_WIDTH = RW_HEADS * RW_HEAD
LORA_W = 64
LORA_A = 64
LORA_G = 128
RW_LN_EPS = 64e-5

HY_COLS = 3 * HY_WIDTH
RW_COLS = 3 * RW_WIDTH + LORA_W + LORA_A + LORA_G
GATE_COLS = 2 * D_MODEL

N_EXPERTS = 16
D_EXPERT = 2048
EC_CAPACITY = 2
NORM_EPS = 1e-6
CAP = EC_CAPACITY * SEQ // N_EXPERTS

NFFT = 2 * SEQ
FFT_R = 128
FFT_H = FFT_R // 2
FFT_J = 8
FFT_C = 512
CHUNK = 64
QUAD = 4
QW = QUAD * RW_HEAD
NQ = RW_HEADS // QUAD

VMEM_LIMIT = 56 * 1024 * 1024


def _cparams(sem, vmem=None, flags=None):
    return pltpu.CompilerParams(dimension_semantics=sem, vmem_limit_bytes=vmem, flags=flags)


def _dot_hi(a, b):
    return jnp.dot(a, b, precision=HI, preferred_element_type=F32)


def _bdot(a, b):
    return jnp.dot(a, b, preferred_element_type=F32)


def _split(a):
    hi = a.astype(BF16)
    return hi, (a - hi.astype(F32)).astype(BF16)


def _dot_c(m, x):
    return _bdot(m, x.astype(BF16))


def _dot_w3(x, w_hi, w_lo):
    xh, xl = _split(x)
    return _bdot(xh, w_hi) + (_bdot(xl, w_hi) + _bdot(xh, w_lo))


def _head_sums(x, ones_bf):
    xh, xl = _split(x)
    return _bdot(xh, ones_bf) + _bdot(xl, ones_bf)


def _rms(x):
    return x * lax.rsqrt(jnp.mean(x * x, axis=-1, keepdims=True) + NORM_EPS)


def _sigmoid(x):
    return 1.0 / (1.0 + jnp.exp(-x))


def _pad2(a, rows, cols):
    return jnp.zeros((rows, cols), F32).at[:a.shape[0], :a.shape[1]].set(a)


def _mod_body(c_ref, w_ref, b_ref, o_ref):
    c = c_ref[...]
    o_ref[...] = _dot_hi(c * _sigmoid(c), w_ref[...]) + b_ref[...]


def _modulation(c8, mod_w, mod_b):
    n = mod_w.shape[1]
    tn = 1536
    return pl.pallas_call(
        _mod_body,
        grid=(n // tn,),
        in_specs=[pl.BlockSpec((8, D_MODEL), lambda j: (0, 0)),
                  pl.BlockSpec((D_MODEL, tn), lambda j: (0, j)),
                  pl.BlockSpec((1, tn), lambda j: (0, j))],
        out_specs=pl.BlockSpec((8, tn), lambda j: (0, j)),
        out_shape=jax.ShapeDtypeStruct((8, n), F32),
        compiler_params=_cparams(("arbitrary",)),
        name="modulation",
    )(c8, mod_w, mod_b.reshape(1, n))


def _inproj_body(nw, x_ref, g_ref, sc_ref, sh_ref, *refs):
    w_refs, o_refs, h_ref = refs[:nw], refs[nw:2 * nw], refs[2 * nw]

    @pl.when(pl.program_id(2) == 0)
    def _():
        h = _rms(x_ref[...]) * g_ref[...] * sc_ref[...] + sh_ref[...]
        h_ref[...] = h.astype(BF16)

    h = h_ref[...]
    for w_ref, o_ref in zip(w_refs, o_refs):
        o_ref[...] = jnp.dot(h, w_ref[...], preferred_element_type=F32).astype(o_ref.dtype)


def _in_projection(x, g, scale1p, shift, weights, tm, nsplit, out_dtypes=None):
    out_dtypes = out_dtypes or [F32] * len(weights)
    bx, tx, _ = x.shape
    nw = len(weights)
    tns = [w.shape[1] // nsplit for w in weights]
    in_specs = [pl.BlockSpec((None, tm, D_MODEL), lambda b, m, n: (b, m, 0)),
                pl.BlockSpec((1, D_MODEL), lambda b, m, n: (0, 0)),
                pl.BlockSpec((None, 1, D_MODEL), lambda b, m, n: (b, 0, 0)),
                pl.BlockSpec((None, 1, D_MODEL), lambda b, m, n: (b, 0, 0))]
    in_specs += [pl.BlockSpec((D_MODEL, tn), lambda b, m, n: (0, n)) for tn in tns]
    out_specs = [pl.BlockSpec((None, tm, tn), lambda b, m, n: (b, m, n)) for tn in tns]
    out_shape = [jax.ShapeDtypeStruct((bx, tx, w.shape[1]), dt) for w, dt in zip(weights, out_dtypes)]
    return pl.pallas_call(
        functools.partial(_inproj_body, nw),
        grid=(bx, tx // tm, nsplit),
        in_specs=in_specs,
        out_specs=out_specs,
        out_shape=out_shape,
        scratch_shapes=[pltpu.VMEM((tm, D_MODEL), BF16)],
        compiler_params=_cparams(("arbitrary", "arbitrary", "arbitrary"), VMEM_LIMIT),
        name="in_projection",
    )(x, g.reshape(1, D_MODEL), scale1p, shift, *weights)


def _neighbours(cur, pv_ref, nx_ref):
    tt = cur.shape[0]
    t = pl.program_id(1)
    first = (t > 0).astype(F32)
    last = (t < pl.num_programs(1) - 1).astype(F32)
    prev_row = pv_ref[7:8, :] * first
    next_row = nx_ref[0:1, :] * last
    rows = lax.broadcasted_iota(I32, cur.shape, 0)
    prev = jnp.where(rows == 0, prev_row, pltpu.roll(cur, 1, 0))
    nxt = jnp.where(rows == tt - 1, next_row, pltpu.roll(cur, tt - 1, 0))
    return prev, nxt


def _halo_specs(tt, t_total, width):
    r8 = tt // 8
    last8 = t_total // 8 - 1
    return [pl.BlockSpec((None, tt, width), lambda b, t: (b, t, 0)),
            pl.BlockSpec((None, 8, width), lambda b, t: (b, jnp.maximum(t * r8 - 1, 0), 0)),
            pl.BlockSpec((None, 8, width), lambda b, t: (b, jnp.minimum((t + 1) * r8, last8), 0))]


def _hy_pre_body(cur_ref, pv_ref, nx_ref, w_ref, b_ref, z_ref, x0_ref):
    cur = cur_ref[...]
    prev, nxt = _neighbours(cur, pv_ref, nx_ref)
    u = w_ref[0:1, :] * prev + w_ref[1:2, :] * cur + w_ref[2:3, :] * nxt + b_ref[...]
    z_ref[...] = u[:, 2 * HY_WIDTH:] * u[:, HY_WIDTH:2 * HY_WIDTH]
    x0_ref[...] = u[:, :HY_WIDTH]


def _hyena_pre(hy_cols, conv_w, conv_b):
    tt = 512
    spec_o = pl.BlockSpec((None, tt, HY_WIDTH), lambda b, t: (b, t, 0))
    return pl.pallas_call(
        _hy_pre_body,
        grid=(BATCH, SEQ // tt),
        in_specs=_halo_specs(tt, SEQ, HY_COLS) + [
            pl.BlockSpec((3, HY_COLS), lambda b, t: (0, 0)),
            pl.BlockSpec((1, HY_COLS), lambda b, t: (0, 0))],
        out_specs=[spec_o, spec_o],
        out_shape=[jax.ShapeDtypeStruct((BATCH, SEQ, HY_WIDTH), F32)] * 2,
        compiler_params=_cparams(("arbitrary", "arbitrary"), VMEM_LIMIT),
        name="hyena_pre",
    )(hy_cols, hy_cols, hy_cols, conv_w, conv_b.reshape(1, HY_COLS))


def _filt_body(bands_ref, w1t_ref, w1c_ref, w1s_ref, b1_ref, w2_ref, b2_ref, w3_ref, fr_ref, dl_ref,
               f_ref, s_ref):
    i = pl.program_id(0)
    tt = f_ref.shape[0]
    rowi = lax.broadcasted_iota(I32, (tt, 1), 0) + i * tt
    rowf = rowi.astype(F32)
    tpos = rowf * (1.0 / (SEQ - 1))
    arg = (rowf * (2.0 * math.pi / SEQ)) * bands_ref[...]
    pre = tpos * w1t_ref[...] + _dot_hi(jnp.cos(arg), w1c_ref[...]) - _dot_hi(jnp.sin(arg), w1s_ref[...]) + b1_ref[...]
    freq = fr_ref[...]
    hid = jnp.sin(freq * pre)
    hid = jnp.sin(freq * (_dot_hi(hid, w2_ref[...]) + b2_ref[...]))
    filt = _dot_hi(hid, w3_ref[...]) * jnp.exp(-tpos * dl_ref[...])
    lane = lax.broadcasted_iota(I32, filt.shape, 1)
    rows = lax.broadcasted_iota(I32, filt.shape, 0) + i * tt
    filt = jnp.where((rows == 0) & (lane >= HY_WIDTH), 0.0, filt)
    f_ref[...] = filt

    @pl.when(i == 0)
    def _():
        s_ref[...] = jnp.zeros_like(s_ref)

    s_ref[...] += jnp.sum(jnp.abs(filt), axis=0, keepdims=True)


def _hyena_filter(w1, b1, w2, b2, w3, freq):
    tt = 512
    bands = np.zeros((1, 128), np.float32)
    bands[0, :HY_BANDS] = np.linspace(1e-4, HY_BANDS - 1, HY_BANDS, dtype=np.float32)
    deltas = np.abs(np.linspace(math.log(HY_DECAY_TARGET) / HY_DECAY_LONG,
                                math.log(HY_DECAY_TARGET) / HY_DECAY_SHORT, HY_WIDTH, dtype=np.float32))
    dl = np.concatenate([deltas, deltas])[None, :].astype(np.float32)
    args = (jnp.asarray(bands),
            _pad2(w1[0:1], 1, 128),
            _pad2(w1[1:1 + HY_BANDS], 128, 128),
            _pad2(w1[1 + HY_BANDS:], 128, 128),
            _pad2(b1[None], 1, 128),
            _pad2(w2, 128, 128),
            _pad2(b2[None], 1, 128),
            _pad2(w3, 128, 2 * HY_WIDTH),
            _pad2(freq[None], 1, 128),
            jnp.asarray(dl))
    full = lambda a: pl.BlockSpec(a.shape, lambda i: (0, 0))
    return pl.pallas_call(
        _filt_body,
        grid=(SEQ // tt,),
        in_specs=[full(a) for a in args],
        out_specs=[pl.BlockSpec((tt, 2 * HY_WIDTH), lambda i: (i, 0)),
                   pl.BlockSpec((1, 2 * HY_WIDTH), lambda i: (0, 0))],
        out_shape=[jax.ShapeDtypeStruct((SEQ, 2 * HY_WIDTH), F32),
                   jax.ShapeDtypeStruct((1, 2 * HY_WIDTH), F32)],
        compiler_params=_cparams(("arbitrary",)),
        name="hyena_filter",
    )(*args)


def _fft_tables():
    k = np.arange(FFT_R)
    n = np.arange(FFT_H)
    th = 2.0 * np.pi * np.outer(k, n) / FFT_R
    f1 = np.stack([np.cos(th), -np.sin(th)])
    eye = np.eye(FFT_J)
    g1 = np.einsum('pkn,jm->pkjnm', f1, eye).reshape(2 * FFT_R * FFT_J, FFT_H * FFT_J)
    f3 = np.stack([np.cos(th.T), -np.sin(th.T)]) / NFFT
    g3 = np.einsum('pnk,jm->njpkm', f3, eye).reshape(FFT_H * FFT_J, 2 * FFT_R * FFT_J)
    th2 = 2.0 * np.pi * np.outer(k, k) / FFT_R
    c, s = np.cos(th2), np.sin(th2)
    fc = np.block([[c, s], [-s, c]])
    fi = np.block([[c, -s], [s, c]])
    return tuple(jnp.asarray(a, F32) for a in (g1, g3, fc, fi))


def _twiddle_tables():
    k = np.arange(FFT_R)
    ang = 2.0 * np.pi * (np.outer(k, k) % NFFT) / NFFT
    first = ang.reshape(FFT_R, FFT_R // FFT_J, FFT_J).transpose(1, 0, 2).reshape(FFT_R // FFT_J, FFT_R * FFT_J, 1)
    mid = ang.reshape(FFT_R // MID_K, MID_K, FFT_R).transpose(0, 2, 1)
    return tuple(jnp.asarray(f(a), F32) for a in (first, mid) for f in (np.cos, np.sin))


def _fft1_body(z_ref, g_ref, cs_ref, sn_ref, o_ref):
    c = z_ref.shape[-1]
    z = z_ref[...].reshape(FFT_H * FFT_J, c)
    a = _dot_c(g_ref[...], z)
    half = FFT_R * FFT_J
    ar, ai = a[:half], a[half:]
    cs, sn = cs_ref[...], sn_ref[...]
    o_ref[0] = (ar * cs + ai * sn).reshape(FFT_R, FFT_J, c)
    o_ref[1] = (ai * cs - ar * sn).reshape(FFT_R, FFT_J, c)


def _fft_stage1(z4, g1, tw):
    bz, _, _, c = z4.shape
    tab = pl.BlockSpec(g1.shape, lambda b, n, cb: (0, 0))
    col = pl.BlockSpec((None, FFT_R * FFT_J, 1), lambda b, n, cb: (n, 0, 0))
    return pl.pallas_call(
        _fft1_body,
        grid=(bz, FFT_R // FFT_J, c // FFT_C),
        in_specs=[pl.BlockSpec((None, FFT_H, FFT_J, FFT_C), lambda b, n, cb: (b, 0, n, cb)), tab, col, col],
        out_specs=pl.BlockSpec((None, 2, FFT_R, FFT_J, FFT_C), lambda b, n, cb: (b, 0, 0, n, cb)),
        out_shape=jax.ShapeDtypeStruct((bz, 2, FFT_R, FFT_R, c), F32),
        compiler_params=_cparams(("arbitrary", "arbitrary", "arbitrary"), VMEM_LIMIT),
        name="fft_stage1",
    )(z4, g1, *tw)


MID_K = 4


def _filt_spec_body(a_ref, f_ref, h_ref):
    for i in range(MID_K):
        x = _dot_c(f_ref[...], jnp.concatenate([a_ref[0, i], a_ref[1, i]], axis=0))
        xr, xi = x[:FFT_R], x[FFT_R:]
        h_ref[i, 0] = xr[:, :HY_WIDTH] + xr[:, HY_WIDTH:]
        h_ref[i, 1] = xi[:, :HY_WIDTH] - xi[:, HY_WIDTH:]


def _filter_spectrum(a, fc):
    tab = pl.BlockSpec(fc.shape, lambda k: (0, 0))
    return pl.pallas_call(
        _filt_spec_body,
        grid=(FFT_R // MID_K,),
        in_specs=[pl.BlockSpec((None, 2, MID_K, FFT_R, 2 * HY_WIDTH), lambda k: (0, 0, k, 0, 0)), tab],
        out_specs=pl.BlockSpec((MID_K, 2, FFT_R, HY_WIDTH), lambda k: (k, 0, 0, 0)),
        out_shape=jax.ShapeDtypeStruct((FFT_R, 2, FFT_R, HY_WIDTH), F32),
        compiler_params=_cparams(("arbitrary",), VMEM_LIMIT),
        name="filter_spectrum",
    )(a, fc)


def _fft_mid_body(a_ref, h_ref, fc_ref, fi_ref, cs_ref, sn_ref, o_ref):
    for i in range(MID_K):
        x = _dot_c(fc_ref[...], jnp.concatenate([a_ref[0, i], a_ref[1, i]], axis=0))
        xr, xi = x[:FFT_R], x[FFT_R:]
        hr, him = h_ref[i, 0], h_ref[i, 1]
        yr = xr * hr - xi * him
        yi = xr * him + xi * hr
        p = _dot_c(fi_ref[...], jnp.concatenate([yr, yi], axis=0))
        pr, pim = p[:FFT_R], p[FFT_R:]
        cs, sn = cs_ref[:, i:i + 1], sn_ref[:, i:i + 1]
        o_ref[0, i] = pr * cs - pim * sn
        o_ref[1, i] = pr * sn + pim * cs


def _fft_mid(a, h, fc, fi, tw):
    bz = a.shape[0]
    blk = pl.BlockSpec((None, 2, MID_K, FFT_R, HY_WIDTH), lambda k, b: (b, 0, k, 0, 0))
    tab = pl.BlockSpec(fc.shape, lambda k, b: (0, 0))
    col = pl.BlockSpec((None, FFT_R, MID_K), lambda k, b: (k, 0, 0))
    return pl.pallas_call(
        _fft_mid_body,
        grid=(FFT_R // MID_K, bz),
        in_specs=[blk, pl.BlockSpec((MID_K, 2, FFT_R, HY_WIDTH), lambda k, b: (k, 0, 0, 0)), tab, tab, col, col],
        out_specs=blk,
        out_shape=jax.ShapeDtypeStruct(a.shape, F32),
        compiler_params=_cparams(("arbitrary", "arbitrary"), VMEM_LIMIT),
        name="fft_mid",
    )(a, h, fc, fi, *tw)


def _ifft_body(q_ref, g_ref, z_ref, x0_ref, inv_ref, hb_ref, o_ref):
    c = HY_WIDTH
    q = q_ref[...].reshape(2 * FFT_R * FFT_J, c)
    conv = _dot_c(g_ref[...], q) * inv_ref[...]
    z = z_ref[...].reshape(FFT_H * FFT_J, c)
    x0 = x0_ref[...].reshape(FFT_H * FFT_J, c)
    o_ref[...] = ((conv + hb_ref[...] * z) * x0).reshape(FFT_H, FFT_J, c)


def _ifft_gate(q, g3, z4, x04, inv_norm, hy_bias):
    row = pl.BlockSpec((None, FFT_H, FFT_J, HY_WIDTH), lambda b, n: (b, 0, n, 0))
    vec = pl.BlockSpec((1, HY_WIDTH), lambda b, n: (0, 0))
    tab = pl.BlockSpec(g3.shape, lambda b, n: (0, 0))
    return pl.pallas_call(
        _ifft_body,
        grid=(BATCH, FFT_R // FFT_J),
        in_specs=[pl.BlockSpec((None, 2, FFT_R, FFT_J, HY_WIDTH), lambda b, n: (b, 0, 0, n, 0)),
                  tab, row, row, vec, vec],
        out_specs=row,
        out_shape=jax.ShapeDtypeStruct(z4.shape, F32),
        compiler_params=_cparams(("arbitrary", "arbitrary"), VMEM_LIMIT),
        name="ifft_gate",
    )(q, g3, z4, x04, inv_norm, hy_bias)


def _hyena_branch(hy_cols, conv_w, conv_b, w1, b1, w2, b2, w3, freq, hy_bias):
    g1, g3, fc, fi = (t.astype(BF16) for t in _fft_tables())
    z, x0 = _hyena_pre(hy_cols, conv_w, conv_b)
    filt, l1 = _hyena_filter(w1, b1, w2, b2, w3, freq)
    inv_norm = 1.0 / (l1[:, :HY_WIDTH] + l1[:, HY_WIDTH:])
    tw1c, tw1s, tw2c, tw2s = _twiddle_tables()
    fa = _fft_stage1(filt.reshape(1, FFT_H, FFT_R, 2 * HY_WIDTH), g1, (tw1c, tw1s))
    h = _filter_spectrum(fa, fc)
    z4 = z.reshape(BATCH, FFT_H, FFT_R, HY_WIDTH)
    za = _fft_stage1(z4, g1, (tw1c, tw1s))
    q = _fft_mid(za, h, fc, fi, (tw2c, tw2s))
    out = _ifft_gate(q, g3, z4, x0.reshape(z4.shape), inv_norm, hy_bias.reshape(1, HY_WIDTH))
    return out.reshape(BATCH, SEQ, HY_WIDTH)


def _rw_prep_body(cur_ref, pv_ref, nx_ref, mu_ref, w2h_ref, w2l_ref, w0_ref, a0_ref, guph_ref, gupl_ref, kkw_ref,
                  ka_ref, rk_ref, ones_ref, r_o, v_o, kk_o, e_o, km_o, b_o, bonus_o, g_o):
    cur = cur_ref[...]
    prev, nxt = _neighbours(cur, pv_ref, nx_ref)
    z = cur + mu_ref[0:1, :] * (prev - cur) + mu_ref[1:2, :] * (nxt - cur)
    w_ = RW_WIDTH
    r, k, v = z[:, :w_], z[:, w_:2 * w_], z[:, 2 * w_:3 * w_]
    ll = z[:, 3 * w_:3 * w_ + 128]
    lg = z[:, 3 * w_ + 128:]
    lane = lax.broadcasted_iota(I32, ll.shape, 1)
    tl = jnp.where(lane < LORA_W, jnp.tanh(ll), ll)
    ones = ones_ref[...]
    kkr = k * kkw_ref[...]
    kk = kkr * lax.rsqrt(jnp.maximum(_head_sums(kkr * kkr, ones), 1e-24))
    bonus = jnp.zeros_like(r)
    for d in range(2):
        proj = _dot_w3(tl, w2h_ref[d], w2l_ref[d])
        wl = proj[:, :w_] + w0_ref[d:d + 1, :]
        al = proj[:, w_:] + a0_ref[d:d + 1, :]
        nwl = -wl
        w = -(jnp.maximum(nwl, 0.0) + jnp.log(1.0 + jnp.exp(-jnp.abs(nwl)))) - 0.5
        a = _sigmoid(al)
        km = k * (1.0 + (a - 1.0) * ka_ref[...])
        e_o[d] = jnp.exp(w)
        km_o[d] = km
        b_o[d] = kk * a
        bonus = bonus + _head_sums(r * km * rk_ref[...], ones)
    r_o[...] = r
    v_o[...] = v.astype(BF16)
    kk_o[...] = kk
    bonus_o[...] = (bonus * v).astype(BF16)
    g_o[...] = _dot_w3(_sigmoid(lg), guph_ref[...], gupl_ref[...]).astype(BF16)


def _head_ones():
    h = np.arange(RW_WIDTH) // RW_HEAD
    return jnp.asarray((h[:, None] == h[None, :]).astype(np.float32), BF16)


def _rwkv_prep(rw_cols, mu, w_up, a_up, w0, a0, g_up, k_k, k_a, r_k):
    bx, tx, _ = rw_cols.shape
    tt = min(512, tx)
    w2 = jnp.zeros((2, 128, 2 * RW_WIDTH), F32)
    w2 = w2.at[:, :LORA_W, :RW_WIDTH].set(w_up).at[:, LORA_W:, RW_WIDTH:].set(a_up)
    vec = lambda a: a.reshape(1, RW_WIDTH)
    full2 = lambda a: pl.BlockSpec(a.shape, lambda b, t: (0,) * a.ndim)
    args = (mu, *_split(w2), w0, a0, *_split(g_up), vec(k_k), vec(k_a), vec(r_k), _head_ones())
    one = pl.BlockSpec((None, tt, RW_WIDTH), lambda b, t: (b, t, 0))
    two = pl.BlockSpec((2, None, tt, RW_WIDTH), lambda b, t: (0, b, t, 0))
    s1 = jax.ShapeDtypeStruct((bx, tx, RW_WIDTH), F32)
    s1h = jax.ShapeDtypeStruct((bx, tx, RW_WIDTH), BF16)
    s2 = jax.ShapeDtypeStruct((2, bx, tx, RW_WIDTH), F32)
    return pl.pallas_call(
        _rw_prep_body,
        grid=(bx, tx // tt),
        in_specs=_halo_specs(tt, tx, RW_COLS) + [full2(a) for a in args],
        out_specs=[one, one, one, two, two, two, one, one],
        out_shape=[s1, s1h, s1, s2, s2, s2, s1h, s1h],
        compiler_params=_cparams(("arbitrary", "arbitrary"), VMEM_LIMIT),
        name="rwkv_prep",
    )(rw_cols, rw_cols, rw_cols, *args)


def _chunk_cumsum(e, fwd):
    row = lax.broadcasted_iota(I32, e.shape, 0)
    c = e
    s = 1
    while s < CHUNK:
        if fwd:
            c = c + jnp.where(row >= s, pltpu.roll(c, s, 0), 0.0)
        else:
            c = c + jnp.where(row < CHUNK - s, pltpu.roll(c, CHUNK - s, 0), 0.0)
        s *= 2
    return c


def _block_diag(x4, lane_head):
    return jnp.concatenate([jnp.where(lane_head == h, x4, 0.0) for h in range(QUAD)], axis=0).astype(BF16)


def _mm(a, b):
    return jnp.dot(a.astype(BF16), b, preferred_element_type=F32)


def _mm_nt(a, b):
    return lax.dot_general(a.astype(BF16), b, (((1,), (1,)), ((), ())), preferred_element_type=F32)


def _scan_chain(fwd, r, v, kk, e, km, bv, s_ref, y_ref):
    n = CHUNK
    cum = _chunk_cumsum(e, fwd)
    tot = jnp.sum(e, axis=0, keepdims=True)
    grow = jnp.exp(cum)
    at = -kk * jnp.exp(e - cum)
    bt = bv * grow
    kt = km * grow
    rt = r * jnp.exp(-cum)
    tail = jnp.exp(cum - tot)
    bb = bv * tail
    kb = km * tail
    p_all = jnp.exp(-tot)

    row = lax.broadcasted_iota(I32, (n, QW), 0)
    lane = lax.broadcasted_iota(I32, (n, QW), 1)
    col = lane & (RW_HEAD - 1)
    lane_head = lane >> 6
    incl = (row >= col) if fwd else (row <= col)
    strict = (row > col) if fwd else (row < col)
    eye = jnp.where(row == col, 1.0, 0.0)
    bd = lambda x4: _block_diag(x4, lane_head)

    def quad_steps(q):
        ql = slice(QW * q, QW * (q + 1))
        a4, v4 = at[:, ql], v[:, ql]
        ar = jnp.concatenate([a4, rt[:, ql]], axis=0)
        bd_v = bd(v4)
        gb = _mm_nt(ar, bd(bt[:, ql]))
        gk = _mm_nt(ar, bd(kt[:, ql]))
        yield
        m = jnp.where(strict, gb[:n], 0.0)
        g_rb = jnp.where(incl, gb[n:], 0.0)
        g_rk = jnp.where(incl, gk[n:], 0.0)
        w4 = _mm(jnp.where(strict, gk[:n], 0.0), bd_v)
        t4 = eye + m
        p = _mm(m, bd(m))
        s4 = s_ref[q]
        sa = _mm_nt(ar, bd(s4))
        yield
        for level in range(1, 6):
            bd_p = bd(p)
            if level < 5:
                both = _mm(jnp.concatenate([p, t4], axis=0), bd_p)
                p = both[:n]
                t4 = t4 + both[n:]
            else:
                t4 = t4 + _mm(t4, bd_p)
            yield
        u4 = _mm(t4, bd(sa[:n] + w4))
        yield
        y_ref[:, ql] = sa[n:] + _mm(g_rb, bd(u4)) + _mm(g_rk, bd_v)
        uv = jnp.concatenate([u4, v4], axis=0).astype(BF16)
        bk = jnp.concatenate([bb[:, ql], kb[:, ql]], axis=0).astype(BF16)
        full = lax.dot_general(uv, bk, (((0,), (0,)), ((), ())), preferred_element_type=F32)
        upd = s4 * p_all[:, ql]
        for h in range(QUAD):
            upd = upd + jnp.where(lane_head == h, full[RW_HEAD * h:RW_HEAD * (h + 1)], 0.0)
        s_ref[q] = upd

    return [quad_steps(q) for q in range(NQ)]


def _round_robin(gens):
    gens = list(gens)
    while gens:
        for g in list(gens):
            try:
                next(g)
            except StopIteration:
                gens.remove(g)


def _rw_scan_body(nb, rf_ref, vf_ref, kf_ref, rb_ref, vb_ref, kb_ref, ef_ref, mf_ref, bf_ref, eb_ref, mb_ref, bb_ref,
                  s0_ref, yf_ref, yb_ref, sf_ref, s_scr):
    c = pl.program_id(1)

    @pl.when(c == 0)
    def _():
        s_scr[...] = s0_ref[...]

    steps = []
    for i in range(nb):
        steps += _scan_chain(True, rf_ref[i], vf_ref[i].astype(F32), kf_ref[i], ef_ref[i], mf_ref[i], bf_ref[i],
                             s_scr.at[0, i], yf_ref.at[i])
        steps += _scan_chain(False, rb_ref[i], vb_ref[i].astype(F32), kb_ref[i], eb_ref[i], mb_ref[i], bb_ref[i],
                             s_scr.at[1, i], yb_ref.at[i])
    _round_robin(steps)

    @pl.when(c == pl.num_programs(1) - 1)
    def _():
        sf_ref[...] = s_scr[...]


def _rwkv_scan(r, v, kk, e, km, bvec, s0):
    bx, tx, _ = r.shape
    nc = tx // CHUNK
    nb = 4
    fpos = lambda g, c: (g, c, 0)
    bpos = lambda g, c: (g, nc - 1 - c, 0)
    tok = lambda f: pl.BlockSpec((nb, CHUNK, RW_WIDTH), f)
    tokd = lambda d, f: pl.BlockSpec((None, nb, CHUNK, RW_WIDTH), lambda g, c: (d,) + f(g, c))
    st = pl.BlockSpec((2, nb, NQ, RW_HEAD, QW), lambda g, c: (0, g, 0, 0, 0))
    y_shape = jax.ShapeDtypeStruct((bx, tx, RW_WIDTH), F32)
    return pl.pallas_call(
        functools.partial(_rw_scan_body, nb),
        grid=(bx // nb, nc),
        in_specs=[tok(fpos)] * 3 + [tok(bpos)] * 3 + [tokd(0, fpos)] * 3 + [tokd(1, bpos)] * 3 + [st],
        out_specs=[tok(fpos), tok(bpos), st],
        out_shape=[y_shape, y_shape, jax.ShapeDtypeStruct((2, bx, NQ, RW_HEAD, QW), F32)],
        scratch_shapes=[pltpu.VMEM((2, nb, NQ, RW_HEAD, QW), F32)],
        compiler_params=_cparams(("arbitrary", "arbitrary"), VMEM_LIMIT),
        name="rwkv_scan",
    )(r, v, kk, r, v, kk, e, km, bvec, e, km, bvec, s0)


def _merge_body(hy_ref, yf_ref, yb_ref, bonus_ref, g_ref, gate_ref, x_ref, hyp_ref, rwp_ref, wo_ref, ones_ref,
                lng_ref, lnb_ref, m2_ref, n1_ref, n2_ref, sc4_ref, m3_ref, rwh_ref, rwl_ref,
                x1_ref, h2_ref, aff_ref, afft_ref):
    ones = ones_ref[...]
    ysum = yf_ref[...] + yb_ref[...]
    mean = _head_sums(ysum, ones) * (1.0 / RW_HEAD)
    dev = ysum - mean
    var = _head_sums(dev * dev, ones) * (1.0 / RW_HEAD)
    yn = dev * lax.rsqrt(var + RW_LN_EPS) * lng_ref[...] + lnb_ref[...]
    y_rw = (yn + bonus_ref[...].astype(F32)) * g_ref[...].astype(F32)
    gates = _sigmoid(gate_ref[...].astype(F32))
    mix = (gates[:, :D_MODEL] * _bdot(hy_ref[...].astype(BF16), hyp_ref[...])
           + gates[:, D_MODEL:] * _bdot(y_rw.astype(BF16), rwp_ref[...]))
    mixed = _bdot(mix.astype(BF16), wo_ref[...])
    x1 = x_ref[...] + m2_ref[...] * (_rms(mixed) * n1_ref[...])
    x1_ref[...] = x1
    h2 = _rms(x1) * n2_ref[...] * sc4_ref[...] + m3_ref[...]
    h2_ref[...] = h2
    logits = _dot_w3(h2, rwh_ref[...], rwl_ref[...])
    lane = lax.broadcasted_iota(I32, logits.shape, 1)
    logits = jnp.where(lane < N_EXPERTS, logits, -1e30)
    ex = jnp.exp(logits - jnp.max(logits, axis=1, keepdims=True))
    aff = ex / jnp.sum(ex, axis=1, keepdims=True)
    aff_ref[...] = aff
    afft_ref[...] = jnp.transpose(aff)[:N_EXPERTS, :]


def _merge(out_hy, y_f, y_b, bonus, g, gate_cols, x, hy_proj, rw_proj, w_out, ln_g, ln_b, m2, n1post, n2pre, sc4, m3,
           router_w):
    tt = 512
    tok = lambda w: pl.BlockSpec((None, tt, w), lambda b, t: (b, t, 0))
    full = lambda a: pl.BlockSpec(a.shape, lambda b, t: (0,) * a.ndim)
    per_b = pl.BlockSpec((None, 1, D_MODEL), lambda b, t: (b, 0, 0))
    vec = lambda a, n: a.reshape(1, n)
    consts = (hy_proj.astype(BF16), rw_proj.astype(BF16), w_out.astype(BF16), _head_ones(),
              vec(ln_g, RW_WIDTH), vec(ln_b, RW_WIDTH))
    router = _split(_pad2(router_w, D_MODEL, 128))
    return pl.pallas_call(
        _merge_body,
        grid=(BATCH, SEQ // tt),
        in_specs=[tok(HY_WIDTH), tok(RW_WIDTH), tok(RW_WIDTH), tok(RW_WIDTH), tok(RW_WIDTH), tok(GATE_COLS),
                  tok(D_MODEL)]
                 + [full(a) for a in consts]
                 + [per_b, full(vec(n1post, D_MODEL)), full(vec(n2pre, D_MODEL)), per_b, per_b,
                    full(router[0]), full(router[1])],
        out_specs=[tok(D_MODEL), tok(D_MODEL), tok(128),
                   pl.BlockSpec((None, N_EXPERTS, tt), lambda b, t: (b, 0, t))],
        out_shape=[jax.ShapeDtypeStruct((BATCH, SEQ, D_MODEL), F32),
                   jax.ShapeDtypeStruct((BATCH, SEQ, D_MODEL), F32),
                   jax.ShapeDtypeStruct((BATCH, SEQ, 128), F32),
                   jax.ShapeDtypeStruct((BATCH, N_EXPERTS, SEQ), F32)],
        compiler_params=_cparams(("arbitrary", "arbitrary"), VMEM_LIMIT),
        name="merge_router",
    )(out_hy, y_f, y_b, bonus, g, gate_cols, x, *consts, m2, vec(n1post, D_MODEL), vec(n2pre, D_MODEL), sc4, m3,
      *router)


TBLK = 128


def _select_body(afft_ref, key_ref, st_ref):
    ne = N_EXPERTS

    def bisect(_, carry):
        lo, hi = carry
        mid = lo + ((hi - lo) >> 1)
        n_ge = jnp.sum(jnp.where(afft_ref[...] >= pltpu.bitcast(mid, F32), 1.0, 0.0), axis=1, keepdims=True)
        ok = n_ge >= float(CAP)
        return jnp.where(ok, mid, lo), jnp.where(ok, hi, mid)

    lo0 = jnp.zeros((ne, 1), I32)
    hi0 = jnp.full((ne, 1), 0x7F800000, I32)
    thr_bits, _ = lax.fori_loop(0, 32, bisect, (lo0, hi0))
    thr = pltpu.bitcast(thr_bits, F32)
    need = float(CAP) - jnp.sum(jnp.where(afft_ref[...] > thr, 1.0, 0.0), axis=1, keepdims=True)

    ri = lax.broadcasted_iota(I32, (128, 128), 0)
    ci = lax.broadcasted_iota(I32, (128, 128), 1)
    upper = jnp.where(ri <= ci, 1.0, 0.0).astype(BF16)
    off_eq = jnp.zeros((ne, 1), F32)
    off_sel = jnp.zeros((ne, 1), F32)
    lane = lax.broadcasted_iota(I32, (ne, 128), 1)
    starts = jnp.zeros((ne, 128), F32)
    for blk in range(SEQ // TBLK):
        sl = slice(TBLK * blk, TBLK * (blk + 1))
        a_b = afft_ref[:, sl]
        eq_b = jnp.where(a_b == thr, 1.0, 0.0)
        inc = _bdot(eq_b.astype(BF16), upper) + off_eq
        off_eq = inc[:, 127:128]
        sel_b = jnp.where(a_b > thr, 1.0, 0.0) + eq_b * jnp.where(inc - eq_b < need, 1.0, 0.0)
        starts = jnp.where(lane == blk, off_sel, starts)
        cnt = _bdot(sel_b.astype(BF16), upper) + off_sel
        off_sel = cnt[:, 127:128]
        key_ref[:, sl] = cnt * sel_b
    st_ref[...] = starts.astype(I32)


def _compact_body(st_ref, key_ref, aff_ref, out_ref):
    ne = N_EXPERTS
    win = 2 * TBLK
    out_ref[...] = jnp.zeros_like(out_ref)
    lane = lax.broadcasted_iota(I32, (TBLK, 128), 1)
    pos = lax.broadcasted_iota(I32, (TBLK, 128), 0).astype(F32)
    slot_off = lax.broadcasted_iota(I32, (win, TBLK), 0).astype(F32) + 1.0
    res_expert = lax.broadcasted_iota(I32, (win, 128), 1) & (ne - 1)

    def per_block(blk, carry):
        o = pl.multiple_of(blk * TBLK, TBLK)
        a0 = aff_ref[pl.ds(o, TBLK), :]
        a_hi = a0.astype(BF16).astype(F32)
        a1 = a0 - a_hi
        a_mid = a1.astype(BF16).astype(F32)
        a_lo = a1 - a_mid
        blk_f = (blk + jnp.zeros((TBLK, 128), I32)).astype(F32)
        payload = jnp.where(lane < ne, pos, jnp.where(lane < 2 * ne, blk_f, jnp.where(
            lane < 3 * ne, pltpu.roll(a_hi, 2 * ne, 1), jnp.where(
                lane < 4 * ne, pltpu.roll(a_mid, 3 * ne, 1),
                jnp.where(lane < 5 * ne, pltpu.roll(a_lo, 4 * ne, 1), 0.0))))).astype(BF16)
        for ex in range(ne):
            c0 = st_ref[ex, blk]
            base = pl.multiple_of(jnp.minimum(c0 & (-TBLK), CAP - win), TBLK)
            slot1 = slot_off + base.astype(F32)
            onehot = jnp.where(key_ref[ex:ex + 1, pl.ds(o, TBLK)] == slot1, 1.0, 0.0).astype(BF16)
            moved = _bdot(onehot, payload)
            rows = pl.ds(base, win)
            out_ref[rows, :] = out_ref[rows, :] + jnp.where(res_expert == ex, moved, 0.0)
        return carry

    lax.fori_loop(0, SEQ // TBLK, per_block, 0)


def _expert_choice(aff_t, aff):
    key, starts = pl.pallas_call(
        _select_body,
        grid=(BATCH,),
        in_specs=[pl.BlockSpec((None, N_EXPERTS, SEQ), lambda b: (b, 0, 0))],
        out_specs=[pl.BlockSpec((None, N_EXPERTS, SEQ), lambda b: (b, 0, 0)),
                   pl.BlockSpec((None, N_EXPERTS, 128), lambda b: (b, 0, 0))],
        out_shape=[jax.ShapeDtypeStruct((BATCH, N_EXPERTS, SEQ), F32),
                   jax.ShapeDtypeStruct((BATCH, N_EXPERTS, 128), I32)],
        compiler_params=_cparams(("arbitrary",)),
        name="expert_select",
    )(aff_t)
    return pl.pallas_call(
        _compact_body,
        grid=(BATCH,),
        in_specs=[pl.BlockSpec((None, N_EXPERTS, 128), lambda b: (b, 0, 0), memory_space=pltpu.SMEM),
                  pl.BlockSpec((None, N_EXPERTS, SEQ), lambda b: (b, 0, 0)),
                  pl.BlockSpec((None, SEQ, 128), lambda b: (b, 0, 0))],
        out_specs=pl.BlockSpec((None, CAP, 128), lambda b: (b, 0, 0)),
        out_shape=jax.ShapeDtypeStruct((BATCH, CAP, 128), F32),
        compiler_params=_cparams(("arbitrary",), VMEM_LIMIT),
        name="expert_compact",
    )(starts, key, aff)


def _gather_body(idx_ref, h_ref, o_ref, buf):
    sub = lax.broadcasted_iota(I32, (8, D_MODEL), 0)

    def group(gi, carry):
        base = pl.multiple_of(gi * 8, 8)
        acc = jnp.zeros((8, D_MODEL), F32)
        for j in range(8):
            row = idx_ref[0, base + j]
            tile = h_ref[pl.ds(pl.multiple_of((row >> 3) << 3, 8), 8), :]
            acc = jnp.where(sub == j, pltpu.roll(tile, (j - row) & 7, 0), acc)
        buf[pl.ds(base, 8), :] = acc
        return carry

    lax.fori_loop(0, CAP // 8, group, 0, unroll=2)
    o_ref[...] = buf[...].astype(BF16)


def _gather(idx3, h2):
    return pl.pallas_call(
        _gather_body,
        grid=(BATCH, N_EXPERTS),
        in_specs=[pl.BlockSpec((None, 1, CAP), lambda b, e: (b * N_EXPERTS + e, 0, 0), memory_space=pltpu.SMEM),
                  pl.BlockSpec((None, SEQ, D_MODEL), lambda b, e: (b, 0, 0), pipeline_mode=pl.Buffered(1))],
        out_specs=pl.BlockSpec((None, None, CAP, D_MODEL), lambda b, e: (b, e, 0, 0)),
        out_shape=jax.ShapeDtypeStruct((BATCH, N_EXPERTS, CAP, D_MODEL), BF16),
        scratch_shapes=[pltpu.VMEM((CAP, D_MODEL), F32)],
        compiler_params=_cparams(("arbitrary", "arbitrary"), VMEM_LIMIT),
        name="moe_gather",
    )(idx3, h2)


def _ffn_body(x_ref, w1_ref, w3_ref, w2_ref, gt_ref, o_ref):
    f = pl.program_id(2)
    x = x_ref[...]
    a = jnp.dot(x, w1_ref[...].astype(BF16), preferred_element_type=F32)
    b = jnp.dot(x, w3_ref[...].astype(BF16), preferred_element_type=F32)
    hid = (a * _sigmoid(a) * b).astype(BF16)
    part = jnp.dot(hid, w2_ref[...].astype(BF16), preferred_element_type=F32)

    @pl.when(f == 0)
    def _():
        o_ref[...] = part

    @pl.when(f > 0)
    def _():
        o_ref[...] += part

    @pl.when(f == pl.num_programs(2) - 1)
    def _():
        o_ref[...] = o_ref[...] * gt_ref[...]


def _experts(xe, w1, w3, w2, gates4):
    tf = 1024
    return pl.pallas_call(
        _ffn_body,
        grid=(BATCH, N_EXPERTS, D_EXPERT // tf),
        in_specs=[pl.BlockSpec((None, None, CAP, D_MODEL), lambda b, e, f: (b, e, 0, 0)),
                  pl.BlockSpec((None, D_MODEL, tf), lambda b, e, f: (e, 0, f)),
                  pl.BlockSpec((None, D_MODEL, tf), lambda b, e, f: (e, 0, f)),
                  pl.BlockSpec((None, tf, D_MODEL), lambda b, e, f: (e, f, 0)),
                  pl.BlockSpec((None, None, CAP, 1), lambda b, e, f: (b, e, 0, 0))],
        out_specs=pl.BlockSpec((None, None, CAP, D_MODEL), lambda b, e, f: (b, e, 0, 0)),
        out_shape=jax.ShapeDtypeStruct((BATCH, N_EXPERTS, CAP, D_MODEL), F32),
        compiler_params=_cparams(("arbitrary", "arbitrary", "arbitrary"), VMEM_LIMIT),
        name="moe_experts",
    )(xe, w1, w3, w2, gates4)


def _scatter_body(idx_ref, ye_ref, o_ref):
    @pl.when(pl.program_id(1) == 0)
    def _():
        o_ref[...] = jnp.zeros_like(o_ref)

    sub = lax.broadcasted_iota(I32, (8, D_MODEL), 0)

    def group(gi, carry):
        base = pl.multiple_of(gi * 8, 8)
        rows8 = ye_ref[pl.ds(base, 8), :]
        for j in range(8):
            row = idx_ref[0, base + j]
            tile = pl.ds(pl.multiple_of((row >> 3) << 3, 8), 8)
            add = jnp.where(sub == (row & 7), jnp.broadcast_to(rows8[j:j + 1, :], (8, D_MODEL)), 0.0)
            o_ref[tile, :] = o_ref[tile, :] + add
        return carry

    lax.fori_loop(0, CAP // 8, group, 0, unroll=2)


def _scatter(idx3, ye):
    return pl.pallas_call(
        _scatter_body,
        grid=(BATCH, N_EXPERTS),
        in_specs=[pl.BlockSpec((None, 1, CAP), lambda b, e: (b * N_EXPERTS + e, 0, 0), memory_space=pltpu.SMEM),
                  pl.BlockSpec((None, None, CAP, D_MODEL), lambda b, e: (b, e, 0, 0))],
        out_specs=pl.BlockSpec((None, SEQ, D_MODEL), lambda b, e: (b, 0, 0), pipeline_mode=pl.Buffered(1)),
        out_shape=jax.ShapeDtypeStruct((BATCH, SEQ, D_MODEL), F32),
        compiler_params=_cparams(("arbitrary", "arbitrary"), VMEM_LIMIT),
        name="moe_scatter",
    )(idx3, ye)


def _final_body(x1_ref, moe_ref, m5_ref, n_ref, o_ref):
    o_ref[...] = x1_ref[...] + m5_ref[...] * (_rms(moe_ref[...]) * n_ref[...])


def _final(x1, moe, m5, n2post):
    tt = 512
    tok = pl.BlockSpec((None, tt, D_MODEL), lambda b, t: (b, t, 0))
    return pl.pallas_call(
        _final_body,
        grid=(BATCH, SEQ // tt),
        in_specs=[tok, tok, pl.BlockSpec((None, 1, D_MODEL), lambda b, t: (b, 0, 0)),
                  pl.BlockSpec((1, D_MODEL), lambda b, t: (0, 0))],
        out_specs=tok,
        out_shape=jax.ShapeDtypeStruct((BATCH, SEQ, D_MODEL), F32),
        compiler_params=_cparams(("arbitrary", "arbitrary")),
        name="moe_residual",
    )(x1, moe, m5, n2post.reshape(1, D_MODEL))


def _layer(x, c, ctx, c_ctx, p):
    c8 = jnp.zeros((8, D_MODEL), F32).at[:BATCH].set(c).at[BATCH].set(c_ctx)
    mod = _modulation(c8, p['mod_w'], p['mod_b'])
    m_lat = [mod[:BATCH, i * D_MODEL:(i + 1) * D_MODEL].reshape(BATCH, 1, D_MODEL) for i in range(6)]
    m_ctx = [jnp.broadcast_to(mod[BATCH, i * D_MODEL:(i + 1) * D_MODEL], (BATCH, 1, D_MODEL)) for i in range(2)]

    w_in = p['w_in'].astype(BF16)
    w_hy = w_in[:, :HY_COLS]
    w_rw = w_in[:, HY_COLS:HY_COLS + RW_COLS]
    w_gate = w_in[:, HY_COLS + RW_COLS:]
    rw_args = (p['rw_mu'], p['rw_w_up'], p['rw_a_up'], p['rw_w0'], p['rw_a0'], p['rw_g_up'],
               p['rw_k_k'], p['rw_k_a'], p['rw_r_k'])

    (ctx_cols,) = _in_projection(ctx, p['norm1_pre'], 1.0 + m_ctx[1], m_ctx[0], [w_rw], CTX_LEN, 1)
    cr, cv, ckk, ce, ckm, cb, _, _ = _rwkv_prep(ctx_cols, *rw_args)
    zero_state = jnp.zeros((2, BATCH, NQ, RW_HEAD, QW), F32)
    _, _, ctx_states = _rwkv_scan(cr, cv, ckk, ce, ckm, cb, zero_state)

    hy_cols, rw_cols, gate_cols = _in_projection(x, p['norm1_pre'], 1.0 + m_lat[1], m_lat[0],
                                                 [w_hy, w_rw, w_gate], 1024, 2, [F32, F32, BF16])
    out_hy = _hyena_branch(hy_cols, p['hy_conv_w'], p['hy_conv_b'], p['hy_ffn_w1'], p['hy_ffn_b1'],
                           p['hy_ffn_w2'], p['hy_ffn_b2'], p['hy_ffn_w3'], p['hy_freq'], p['hy_bias'])
    r, v, kk, e, km, bvec, bonus, g = _rwkv_prep(rw_cols, *rw_args)
    y_f, y_b, _ = _rwkv_scan(r, v, kk, e, km, bvec, ctx_states)

    x1, h2, aff, aff_t = _merge(out_hy, y_f, y_b, bonus, g, gate_cols, x, p['hy_proj'], p['rw_proj'], p['w_out'],
                                p['rw_ln_g'], p['rw_ln_b'], m_lat[2], p['norm1_post'], p['norm2_pre'],
                                1.0 + m_lat[4], m_lat[3], p['router_w'])
    picked = _expert_choice(aff_t, aff)
    ne = N_EXPERTS
    idx = (TBLK * picked[:, :, ne:2 * ne] + picked[:, :, :ne]).astype(I32)
    gates = (picked[:, :, 2 * ne:3 * ne] + picked[:, :, 3 * ne:4 * ne]) + picked[:, :, 4 * ne:5 * ne]
    idx3 = jnp.swapaxes(idx, 1, 2).reshape(BATCH * ne, 1, CAP)
    gates4 = jnp.swapaxes(gates, 1, 2).reshape(BATCH, ne, CAP, 1)
    xe = _gather(idx3, h2)
    ye = _experts(xe, p['exp_w1'], p['exp_w3'], p['exp_w2'], gates4)
    moe = _scatter(idx3, ye)
    return _final(x1, moe, m_lat[5], p['norm2_post'])


def kernel(x, c, ctx, c_ctx, mod_w, mod_b, norm1_pre, norm1_post, norm2_pre, norm2_post, w_in, hy_conv_w, hy_conv_b, hy_ffn_w1, hy_ffn_b1, hy_ffn_w2, hy_ffn_b2, hy_ffn_w3, hy_freq, hy_bias, hy_proj, rw_mu, rw_w0, rw_w_up, rw_a0, rw_a_up, rw_g_up, rw_k_k, rw_k_a, rw_r_k, rw_ln_g, rw_ln_b, rw_proj, w_out, router_w, exp_w1, exp_w3, exp_w2):
    names = ('mod_w', 'mod_b', 'norm1_pre', 'norm1_post', 'norm2_pre', 'norm2_post', 'w_in', 'hy_conv_w',
             'hy_conv_b', 'hy_ffn_w1', 'hy_ffn_b1', 'hy_ffn_w2', 'hy_ffn_b2', 'hy_ffn_w3', 'hy_freq', 'hy_bias',
             'hy_proj', 'rw_mu', 'rw_w0', 'rw_w_up', 'rw_a0', 'rw_a_up', 'rw_g_up', 'rw_k_k', 'rw_k_a', 'rw_r_k',
             'rw_ln_g', 'rw_ln_b', 'rw_proj', 'w_out', 'router_w', 'exp_w1', 'exp_w3', 'exp_w2')
    vals = (mod_w, mod_b, norm1_pre, norm1_post, norm2_pre, norm2_post, w_in, hy_conv_w, hy_conv_b, hy_ffn_w1,
            hy_ffn_b1, hy_ffn_w2, hy_ffn_b2, hy_ffn_w3, hy_freq, hy_bias, hy_proj, rw_mu, rw_w0, rw_w_up, rw_a0,
            rw_a_up, rw_g_up, rw_k_k, rw_k_a, rw_r_k, rw_ln_g, rw_ln_b, rw_proj, w_out, router_w, exp_w1, exp_w3,
            exp_w2)
    depth = mod_w.shape[0]
    assert depth == 1, "single-layer block: the context stream only provides scan start states"
    p = {n: a[0] for n, a in zip(names, vals)}
    return _layer(x, c, ctx, c_ctx, p)
```
